```python
import math
import jax, jax.numpy as jnp
from jax import lax
import numpy as np

D_MODEL = 1024
BATCH = 4
SEQ = 8192
DEPTH = 4
DEC_BATCH = 8
DEC_SEQ = 4096
PAST_LEN = 128

GROUP_W = D_MODEL // 4
A_HEADS = 4
A_VDIM = GROUP_W // A_HEADS
A_QK = A_VDIM // 2
A_QBLOCK = 128
B_CH = GROUP_W
B_KERNEL = 31
C_CH = GROUP_W
C_KERNEL = 3
D_HEADS = 4
D_HDIM = GROUP_W // D_HEADS
GRID_W = 64
WIN_ROWS = 8
WIN_COLS = 16
A_COLS = 3 * GROUP_W
B_COLS = 2 * B_CH
C_COLS = 3 * C_CH
D_COLS = 3 * GROUP_W
IN_COLS = A_COLS + B_COLS + C_COLS + D_COLS
N_EXPERTS = 32
TOP_K = 4
D_FF = D_MODEL
SWIGLU_ALPHA = 1.702
SWIGLU_LIMIT = 7.0
MOE_BLOCK = 128
N_MOD = 6
NORM_EPS = 1e-6
LN_EPS = 1e-5

kernel_name = "hymba_style_hybrid_encoder_moe"


def rms_norm(x, g, eps=NORM_EPS):
    xf = x.astype(jnp.float32)
    y = xf * lax.rsqrt(jnp.mean(xf * xf, axis=-1, keepdims=True) + eps)
    return (y * g.astype(jnp.float32)).astype(x.dtype)


def layer_norm(x, g, b, eps=LN_EPS):
    xf = x.astype(jnp.float32)
    mu = jnp.mean(xf, axis=-1, keepdims=True)
    var = jnp.mean(jnp.square(xf - mu), axis=-1, keepdims=True)
    y = (xf - mu) * lax.rsqrt(var + eps) * g.astype(jnp.float32) + b.astype(jnp.float32)
    return y.astype(x.dtype)


def depthwise_conv(x, w):
    k = w.shape[0]
    pad = (k - 1) // 2
    return lax.conv_general_dilated(
        x, w[:, None, :].astype(x.dtype), window_strides=(1,), padding=[(pad, pad)],
        dimension_numbers=('NWC', 'WIO', 'NWC'), feature_group_count=x.shape[-1])


def alibi_slopes(n):
    return jnp.asarray(2.0 ** (-8.0 * np.arange(1, n + 1) / n), dtype=jnp.float32)


def diff_attention(u, q_gain, k_gain, lam_vecs, sub_gain, lambda_init):
    b, t, _ = u.shape
    q, k, v = jnp.split(u, 3, axis=-1)
    q = rms_norm(q.reshape(b, t, A_HEADS, 2, A_QK), q_gain)
    k = rms_norm(k.reshape(b, t, A_HEADS, 2, A_QK), k_gain)
    v = v.reshape(b, t, A_HEADS, A_VDIM)
    lv = lam_vecs.astype(jnp.float32)
    lam = jnp.exp(jnp.sum(lv[0] * lv[1])) - jnp.exp(jnp.sum(lv[2] * lv[3])) + lambda_init
    slopes = alibi_slopes(A_HEADS)
    kpos = jnp.arange(t, dtype=jnp.float32)
    nb = t // A_QBLOCK
    qb = q.reshape(b, nb, A_QBLOCK, A_HEADS, 2, A_QK).transpose(1, 0, 2, 3, 4, 5)
    scale = A_QK ** -0.5

    def block(args):
        qi, idx = args
        s = jnp.einsum('bqhcd,bkhcd->bhcqk', qi, k, preferred_element_type=jnp.float32) * scale
        qpos = (idx * A_QBLOCK + jnp.arange(A_QBLOCK)).astype(jnp.float32)
        dist = jnp.abs(qpos[:, None] - kpos[None, :])
        s = s - slopes[None, :, None, None, None] * dist[None, None, None]
        p = jax.nn.softmax(s, axis=-1)
        w = p[:, :, 0] - lam * p[:, :, 1]
        return jnp.einsum('bhqk,bkhd->bqhd', w.astype(v.dtype), v)

    o = lax.map(block, (qb, jnp.arange(nb)))
    o = o.transpose(1, 0, 2, 3, 4).reshape(b, t, A_HEADS, A_VDIM)
    o = rms_norm(o, sub_gain) * (1.0 - lambda_init)
    return o.reshape(b, t, GROUP_W)


def conformer_conv(u, w_dw, b_dw, ln_g, ln_b):
    a, g = jnp.split(u, 2, axis=-1)
    h = a * jax.nn.sigmoid(g)
    h = depthwise_conv(h, w_dw) + b_dw
    h = layer_norm(h, ln_g, ln_b)
    return jax.nn.silu(h)


def short_conv(u, w_conv):
    bg, cg, h = jnp.split(u, 3, axis=-1)
    return bg * depthwise_conv(cg * h, w_conv)


def neighbourhood_attention(u, q_gain, k_gain, rpb):
    b, t, _ = u.shape
    rows = t // GRID_W
    kh = min(WIN_ROWS, rows)
    kw = WIN_COLS
    q, k, v = jnp.split(u, 3, axis=-1)
    shp = (b, rows, GRID_W, D_HEADS, D_HDIM)
    q = rms_norm(q.reshape(shp), q_gain)
    k = rms_norm(k.reshape(shp), k_gain)
    v = v.reshape(shp)
    cols = np.arange(GRID_W)
    col_start = np.clip(cols - kw // 2, 0, GRID_W - kw)
    col_idx = col_start[:, None] + np.arange(kw)[None, :]
    col_rel = col_idx - cols[:, None] + (WIN_COLS - 1)
    rpb_cols = rpb[:, :, col_rel]
    scale = D_HDIM ** -0.5

    def row_block(args):
        qr, r = args
        rs = jnp.clip(r - kh // 2, 0, rows - kh)
        k_rows = lax.dynamic_slice_in_dim(k, rs, kh, axis=1)
        v_rows = lax.dynamic_slice_in_dim(v, rs, kh, axis=1)
        k_win = k_rows[:, :, col_idx]
        v_win = v_rows[:, :, col_idx]
        row_rel = rs + jnp.arange(kh) - r + (WIN_ROWS - 1)
        bias = jnp.take(rpb_cols, row_rel, axis=1)
        s = jnp.einsum('bchd,bicjhd->bhcij', qr, k_win, preferred_element_type=jnp.float32) * scale
        s = s + bias.transpose(0, 2, 1, 3)[None].astype(jnp.float32)
        p = jax.nn.softmax(s.reshape(b, D_HEADS, GRID_W, kh * kw), axis=-1).reshape(s.shape)
        return jnp.einsum('bhcij,bicjhd->bchd', p.astype(v.dtype), v_win)

    o = lax.map(row_block, (q.transpose(1, 0, 2, 3, 4), jnp.arange(rows)))
    return o.transpose(1, 0, 2, 3, 4).reshape(b, t, GROUP_W)


def clamped_swiglu(h):
    glu, lin = jnp.split(h, 2, axis=-1)
    glu = jnp.minimum(glu, SWIGLU_LIMIT)
    lin = jnp.clip(lin, -SWIGLU_LIMIT, SWIGLU_LIMIT)
    return glu * jax.nn.sigmoid(SWIGLU_ALPHA * glu) * (lin + 1.0)


def moe(x, w_router, b_router, w1, b1, w2, b2):
    b, t, d = x.shape
    n = b * t
    xf = x.reshape(n, d)
    logits = (xf @ w_router + b_router).astype(jnp.float32)
    top_val, top_idx = lax.top_k(logits, TOP_K)
    gates = jax.nn.softmax(top_val, axis=-1).astype(x.dtype)
    nk = n * TOP_K
    e_flat = top_idx.reshape(nk)
    order = jnp.argsort(e_flat)
    e_sorted = e_flat[order]
    tok_sorted = order // TOP_K
    gate_sorted = gates.reshape(nk)[order]
    counts = jnp.zeros((N_EXPERTS,), jnp.int32).at[e_flat].add(1)
    padded = (counts + MOE_BLOCK - 1) // MOE_BLOCK * MOE_BLOCK
    start = jnp.cumsum(counts) - counts
    pend = jnp.cumsum(padded)
    pstart = pend - padded
    dest = pstart[e_sorted] + jnp.arange(nk) - start[e_sorted]
    n_blocks = nk // MOE_BLOCK + N_EXPERTS
    n_rows = n_blocks * MOE_BLOCK
    row_tok = jnp.full((n_rows,), n, jnp.int32).at[dest].set(tok_sorted)
    row_gate = jnp.zeros((n_rows,), x.dtype).at[dest].set(gate_sorted)
    block_expert = jnp.minimum(
        jnp.searchsorted(pend, jnp.arange(n_blocks) * MOE_BLOCK, side='right'), N_EXPERTS - 1)
    x_pad = jnp.concatenate([xf, jnp.zeros((1, d), x.dtype)], axis=0)

    def expert_block(args):
        toks, g, e = args
        xb = x_pad[toks]
        h = xb @ w1[e] + b1[e]
        y = clamped_swiglu(h) @ w2[e] + b2[e]
        return y * g[:, None]

    y = lax.map(expert_block, (row_tok.reshape(n_blocks, MOE_BLOCK),
                               row_gate.reshape(n_blocks, MOE_BLOCK), block_expert))
    out = jnp.zeros((n + 1, d), x.dtype).at[row_tok].add(y.reshape(n_rows, d))
    return out[:n].reshape(b, t, d)


def trunk(x, c, norm1, norm2, w_ada, b_ada, w_in, w_out, a_q_gain, a_k_gain, a_lambda,
          a_sub_gain, b_dw, b_dw_bias, b_ln_g, b_ln_b, c_conv, d_q_gain, d_k_gain, d_rpb,
          w_router, b_router, w1, b1, w2, b2):
    splits = [A_COLS, A_COLS + B_COLS, A_COLS + B_COLS + C_COLS]
    for l in range(DEPTH):
        lambda_init = 0.8 - 0.6 * math.exp(-0.3 * l)
        mod = jax.nn.silu(c) @ w_ada[l] + b_ada[l]
        sh1, sc1, g1, sh2, sc2, g2 = jnp.split(mod[:, None, :], N_MOD, axis=-1)
        h = rms_norm(x, norm1[l]) * (1.0 + sc1) + sh1
        u = h @ w_in[l]
        ua, ub, uc, ud = jnp.split(u, splits, axis=-1)
        oa = diff_attention(ua, a_q_gain[l], a_k_gain[l], a_lambda[l], a_sub_gain[l], lambda_init)
        ob = conformer_conv(ub, b_dw[l], b_dw_bias[l], b_ln_g[l], b_ln_b[l])
        oc = short_conv(uc, c_conv[l])
        od = neighbourhood_attention(ud, d_q_gain[l], d_k_gain[l], d_rpb[l])
        mix = jnp.concatenate([oa, ob, oc, od], axis=-1) @ w_out[l]
        x = x + g1 * mix
        h = rms_norm(x, norm2[l]) * (1.0 + sc2) + sh2
        x = x + g2 * moe(h, w_router[l], b_router[l], w1[l], b1[l], w2[l], b2[l])
    return x


def setup_inputs(seed: int = 0) -> dict:
    key = jax.random.key(seed)
    ks = jax.random.split(key, 32)
    f32 = jnp.float32
    nrm = lambda k, shape, s: jax.random.normal(k, shape, f32) * s
    gain = lambda k, shape: 1.0 + 0.02 * jax.random.normal(k, shape, f32)
    return {
        "x_prompt": nrm(ks[0], (BATCH, SEQ, D_MODEL), 1.0),
        "x_sample": nrm(ks[1], (DEC_BATCH, DEC_SEQ, D_MODEL), 1.0),
        "c_prompt": nrm(ks[2], (BATCH, D_MODEL), 1.0),
        "c_sample": nrm(ks[3], (DEC_BATCH, D_MODEL), 1.0),
        "norm1": gain(ks[4], (DEPTH, D_MODEL)),
        "norm2": gain(ks[5], (DEPTH, D_MODEL)),
        "w_ada": nrm(ks[6], (DEPTH, D_MODEL, N_MOD * D_MODEL), 0.5 * D_MODEL ** -0.5),
        "b_ada": nrm(ks[7], (DEPTH, N_MOD * D_MODEL), 0.02),
        "w_in": nrm(ks[8], (DEPTH, D_MODEL, IN_COLS), D_MODEL ** -0.5),
        "w_out": nrm(ks[9], (DEPTH, D_MODEL, D_MODEL), D_MODEL ** -0.5),
        "a_q_gain": gain(ks[10], (DEPTH, A_QK)),
        "a_k_gain": gain(ks[11], (DEPTH, A_QK)),
        "a_lambda": nrm(ks[12], (DEPTH, 4, A_QK), 0.1),
        "a_sub_gain": gain(ks[13], (DEPTH, A_VDIM)),
        "b_dw": nrm(ks[14], (DEPTH, B_KERNEL, B_CH), B_KERNEL ** -0.5),
        "b_dw_bias": nrm(ks[15], (DEPTH, B_CH), 0.02),
        "b_ln_g": gain(ks[16], (DEPTH, B_CH)),
        "b_ln_b": nrm(ks[17], (DEPTH, B_CH), 0.02),
        "c_conv": nrm(ks[18], (DEPTH, C_KERNEL, C_CH), C_KERNEL ** -0.5),
        "d_q_gain": gain(ks[19], (DEPTH, D_HDIM)),
        "d_k_gain": gain(ks[20], (DEPTH, D_HDIM)),
        "d_rpb": nrm(ks[21], (DEPTH, D_HEADS, 2 * WIN_ROWS - 1, 2 * WIN_COLS - 1), 0.1),
        "w_router": nrm(ks[22], (DEPTH, D_MODEL, N_EXPERTS), D_MODEL ** -0.5),
        "b_router": nrm(ks[23], (DEPTH, N_EXPERTS), 0.01),
        "w1": nrm(ks[24], (DEPTH, N_EXPERTS, D_MODEL, 2 * D_FF), D_MODEL ** -0.5),
        "b1": nrm(ks[25], (DEPTH, N_EXPERTS, 2 * D_FF), 0.02),
        "w2": nrm(ks[26], (DEPTH, N_EXPERTS, D_FF, D_MODEL), D_FF ** -0.5),
        "b2": nrm(ks[27], (DEPTH, N_EXPERTS, D_MODEL), 0.02),
    }


def reference(x_prompt, x_sample, c_prompt, c_sample, norm1, norm2, w_ada, b_ada, w_in, w_out,
              a_q_gain, a_k_gain, a_lambda, a_sub_gain, b_dw, b_dw_bias, b_ln_g, b_ln_b, c_conv,
              d_q_gain, d_k_gain, d_rpb, w_router, b_router, w1, b1, w2, b2):
    y_prompt = trunk(x_prompt, c_prompt, norm1, norm2, w_ada, b_ada, w_in, w_out, a_q_gain,
                     a_k_gain, a_lambda, a_sub_gain, b_dw, b_dw_bias, b_ln_g, b_ln_b, c_conv,
                     d_q_gain, d_k_gain, d_rpb, w_router, b_router, w1, b1, w2, b2)
    y_sample = trunk(x_sample, c_sample, norm1, norm2, w_ada, b_ada, w_in, w_out, a_q_gain,
                     a_k_gain, a_lambda, a_sub_gain, b_dw, b_dw_bias, b_ln_g, b_ln_b, c_conv,
                     d_q_gain, d_k_gain, d_rpb, w_router, b_router, w1, b1, w2, b2)
    return (y_prompt, y_sample)
```

```python
import functools
import math

import numpy as np
import jax
import jax.numpy as jnp
from jax import lax
from jax.experimental import pallas as pl
from jax.experimental.pallas import tpu as pltpu

F32 = jnp.float32
BF16 = jnp.bfloat16

D_MODEL = 1024
GROUP_W = 256
A_HEADS = 4
A_VDIM = 64
A_QK = 32
B_KERNEL = 31
C_KERNEL = 3
D_HEADS = 4
D_HDIM = 64
GRID_W = 64
WIN_ROWS = 8
WIN_COLS = 16
IN_COLS = 11 * GROUP_W
N_EXPERTS = 32
TOP_K = 4
D_FF = 1024
SWIGLU_ALPHA = 1.702
SWIGLU_LIMIT = 7.0
N_MOD = 6
NORM_EPS = 1e-6
LN_EPS = 1e-5

LOG2E = 1.4426950408889634
NEG_BIG = -1e30
HALO = 16
D_QROWS = 4
D_KROWS = 12
D_TQ = D_QROWS * GRID_W
D_TK = D_KROWS * GRID_W
MOE_TM = 512
VMEM_LIMIT = 56 * 1024 * 1024


def _cparams(sems, vmem=VMEM_LIMIT):
    return pltpu.CompilerParams(dimension_semantics=sems, vmem_limit_bytes=vmem)


def _sigmoid(x):
    return 1.0 / (1.0 + jnp.exp(-x))


def _ada_kernel(c_ref, w_ref, b_ref, o_ref):
    c = c_ref[...]
    s = c * _sigmoid(c)
    o_ref[0] = jnp.dot(s.astype(BF16), w_ref[0].astype(BF16),
                       preferred_element_type=F32) + b_ref[0]


def _ada(c_all, w_ada, b_ada):
    depth, d, n6 = w_ada.shape
    rows = c_all.shape[0]
    tn = 1536
    return pl.pallas_call(
        _ada_kernel,
        grid=(depth, n6 // tn),
        in_specs=[pl.BlockSpec((rows, d), lambda l, j: (0, 0)),
                  pl.BlockSpec((1, d, tn), lambda l, j: (l, 0, j)),
                  pl.BlockSpec((1, 1, tn), lambda l, j: (l, 0, j))],
        out_specs=pl.BlockSpec((1, rows, tn), lambda l, j: (l, 0, j)),
        out_shape=jax.ShapeDtypeStruct((depth, rows, n6), F32),
        compiler_params=_cparams(("parallel", "parallel")),
        name="ada_mod",
    )(c_all, w_ada, b_ada.reshape(depth, 1, n6))


def _inproj_kernel(x_ref, g_ref, sh_ref, sc_ref, w_ref, g32_ref, g64_ref,
                   aq_ref, ak_ref, dq_ref, dk_ref,
                   qa_ref, ka_ref, va_ref, ub_ref, uc_ref, qd_ref, kd_ref, vd_ref):
    x = x_ref[...]
    ms = jnp.mean(x * x, axis=-1, keepdims=True)
    h = x * lax.rsqrt(ms + NORM_EPS) * g_ref[...]
    h = h * (1.0 + sc_ref[...]) + sh_ref[...]
    hb = h.astype(BF16)

    def proj(lo, hi):
        return jnp.dot(hb, w_ref[:, lo:hi], preferred_element_type=F32)

    def group_norm(u, gmat_ref, gain_ref, post):
        u2 = u * u
        u2_hi = u2.astype(BF16)
        u2_lo = (u2 - u2_hi.astype(F32)).astype(BF16)
        gmat = gmat_ref[...]
        gms = (jnp.dot(u2_hi, gmat, preferred_element_type=F32)
               + jnp.dot(u2_lo, gmat, preferred_element_type=F32))
        return u * lax.rsqrt(gms + NORM_EPS) * (gain_ref[...] * post)

    w = GROUP_W
    a_scale = (A_QK ** -0.5) * LOG2E
    d_scale = (D_HDIM ** -0.5) * LOG2E
    qa_ref[...] = group_norm(proj(0, w), g32_ref, aq_ref, a_scale).astype(BF16)
    ka_ref[...] = group_norm(proj(w, 2 * w), g32_ref, ak_ref, 1.0).astype(BF16)
    va_ref[...] = proj(2 * w, 3 * w).astype(BF16)
    ub_ref[...] = proj(3 * w, 5 * w)
    uc_ref[...] = proj(5 * w, 8 * w)
    qd_ref[...] = group_norm(proj(8 * w, 9 * w), g64_ref, dq_ref, d_scale).astype(BF16)
    kd_ref[...] = group_norm(proj(9 * w, 10 * w), g64_ref, dk_ref, 1.0).astype(BF16)
    vd_ref[...] = proj(10 * w, 11 * w).astype(BF16)


def _group_mean_matrix(group):
    idx = np.arange(GROUP_W) // group
    return jnp.asarray((idx[:, None] == idx[None, :]).astype(np.float32) / group, dtype=BF16)


def _inproj(x2, t, norm_g, mod4, w_in, aq, ak, dq, dk, tm=512):
    n, d = x2.shape
    tm = min(tm, t)
    tpb = t // tm
    row = lambda i: (i, 0)
    const = lambda i: (0, 0)
    w = GROUP_W
    mod_spec = lambda k: pl.BlockSpec((None, None, 1, d), lambda i: (i // tpb, k, 0, 0))
    out_cols = [(w, BF16), (w, BF16), (w, BF16), (2 * w, F32), (3 * w, F32),
                (w, BF16), (w, BF16), (w, BF16)]
    return pl.pallas_call(
        _inproj_kernel,
        grid=(n // tm,),
        in_specs=[pl.BlockSpec((tm, d), row),
                  pl.BlockSpec((1, d), const),
                  mod_spec(0), mod_spec(1),
                  pl.BlockSpec((d, IN_COLS), const),
                  pl.BlockSpec((w, w), const), pl.BlockSpec((w, w), const),
                  pl.BlockSpec((1, w), const), pl.BlockSpec((1, w), const),
                  pl.BlockSpec((1, w), const), pl.BlockSpec((1, w), const)],
        out_specs=[pl.BlockSpec((tm, c), row) for c, _ in out_cols],
        out_shape=[jax.ShapeDtypeStruct((n, c), dt) for c, dt in out_cols],
        compiler_params=_cparams(("parallel",)),
        name="inproj",
    )(x2, norm_g, mod4, mod4, w_in, _group_mean_matrix(A_QK), _group_mean_matrix(D_HDIM),
      aq, ak, dq, dk)


def _attn_a_kernel(q_ref, kt_ref, v_ref, lam_ref, sg_ref, o_ref, qm_sc, m_sc, acc_sc,
                   *, tq, tk, lambda_init):
    i = pl.program_id(1)
    j = pl.program_id(2)
    nj = pl.num_programs(2)
    n_maps = 2 * A_HEADS

    @pl.when(j == 0)
    def _init():
        q = q_ref[0].astype(F32)
        grp = lax.broadcasted_iota(jnp.int32, q.shape, 1) // A_QK
        for hc in range(n_maps):
            qm_sc[hc] = jnp.where(grp == hc, q, 0.0).astype(BF16)
        m_sc[...] = jnp.full(m_sc.shape, NEG_BIG, F32)
        acc_sc[...] = jnp.zeros(acc_sc.shape, F32)

    kt = kt_ref[0]
    row = lax.broadcasted_iota(jnp.int32, (tq, tk), 0)
    col = lax.broadcasted_iota(jnp.int32, (tq, tk), 1)
    dist = jnp.abs((i * tq - j * tk) + row - col).astype(F32)
    for h in range(A_HEADS):
        slope = 2.0 ** (-8.0 * (h + 1) / A_HEADS)
        bias = dist * (-slope * LOG2E)
        vh = v_ref[0, h]
        for c in range(2):
            hc = 2 * h + c
            s = jnp.dot(qm_sc[hc], kt, preferred_element_type=F32) + bias
            m_prev = m_sc[hc]
            m_new = jnp.maximum(m_prev, jnp.max(s, axis=-1, keepdims=True))
            alpha = jnp.exp2(m_prev - m_new)
            p = jnp.exp2(s - m_new[:, 0:1])
            acc_sc[hc] = alpha * acc_sc[hc] + jnp.dot(p.astype(BF16), vh,
                                                      preferred_element_type=F32)
            m_sc[hc] = m_new

    @pl.when(j == nj - 1)
    def _finish():
        lv = lam_ref[...]
        lam = (jnp.exp(jnp.sum(lv[0:1] * lv[1:2], axis=-1, keepdims=True))
               - jnp.exp(jnp.sum(lv[2:3] * lv[3:4], axis=-1, keepdims=True)) + lambda_init)
        lane = lax.broadcasted_iota(jnp.int32, (tq, 2 * A_VDIM), 1)
        heads = []
        for h in range(A_HEADS):
            a1 = acc_sc[2 * h]
            a2 = acc_sc[2 * h + 1]
            l1 = jnp.sum(jnp.where(lane == A_VDIM, a1, 0.0), axis=-1, keepdims=True)
            l2 = jnp.sum(jnp.where(lane == A_VDIM, a2, 0.0), axis=-1, keepdims=True)
            o = jnp.where(lane < A_VDIM, a1 / l1 - lam * (a2 / l2), 0.0)
            ms = jnp.sum(o * o, axis=-1, keepdims=True) * (1.0 / A_VDIM)
            heads.append(o * lax.rsqrt(ms + NORM_EPS) * (sg_ref[...] * (1.0 - lambda_init)))
        o_ref[0, :, 0:128] = heads[0] + pltpu.roll(heads[1], A_VDIM, 1)
        o_ref[0, :, 128:256] = heads[2] + pltpu.roll(heads[3], A_VDIM, 1)


def _attn_a(qa, kt, v_aug, lam, sub_gain, lambda_init, tq=256, tk=1024):
    b, t, w = qa.shape
    tq = min(tq, t)
    tk = min(tk, t)
    n_maps = 2 * A_HEADS
    return pl.pallas_call(
        functools.partial(_attn_a_kernel, tq=tq, tk=tk, lambda_init=lambda_init),
        grid=(b, t // tq, t // tk),
        in_specs=[pl.BlockSpec((1, tq, w), lambda bi, i, j: (bi, i, 0)),
                  pl.BlockSpec((1, w, tk), lambda bi, i, j: (bi, 0, j)),
                  pl.BlockSpec((1, A_HEADS, tk, 2 * A_VDIM), lambda bi, i, j: (bi, 0, j, 0)),
                  pl.BlockSpec((4, A_QK), lambda bi, i, j: (0, 0)),
                  pl.BlockSpec((1, 2 * A_VDIM), lambda bi, i, j: (0, 0))],
        out_specs=pl.BlockSpec((1, tq, w), lambda bi, i, j: (bi, i, 0)),
        out_shape=jax.ShapeDtypeStruct((b, t, w), F32),
        scratch_shapes=[pltpu.VMEM((n_maps, tq, w), BF16),
                        pltpu.VMEM((n_maps, tq, 128), F32),
                        pltpu.VMEM((n_maps, tq, 2 * A_VDIM), F32)],
        compiler_params=_cparams(("parallel", "parallel", "arbitrary")),
        name="attn_a",
    )(qa, kt, v_aug, lam, sub_gain)


def _conv_kernel(ub_ref, ubp_ref, ubn_ref, uc_ref, ucp_ref, ucn_ref,
                 wdw_ref, bdw_ref, lng_ref, lnb_ref, wc_ref,
                 ob_ref, oc_ref, gext, pext, *, tt, rc):
    i = pl.program_id(1)
    n = pl.num_programs(1)
    has_prev = jnp.where(i > 0, 1.0, 0.0)
    has_next = jnp.where(i < n - 1, 1.0, 0.0)
    w = GROUP_W

    def glu(u):
        return u[:, :w] * _sigmoid(u[:, w:])

    def gated(u):
        return u[:, w:2 * w] * u[:, 2 * w:]

    gext[0:HALO, :] = glu(ubp_ref[0]) * has_prev
    gext[HALO:HALO + tt, :] = glu(ub_ref[0])
    gext[HALO + tt:2 * HALO + tt, :] = glu(ubn_ref[0]) * has_next
    pext[0:HALO, :] = gated(ucp_ref[0]) * has_prev
    pext[HALO:HALO + tt, :] = gated(uc_ref[0])
    pext[HALO + tt:2 * HALO + tt, :] = gated(ucn_ref[0]) * has_next

    pad_b = (B_KERNEL - 1) // 2
    pad_c = (C_KERNEL - 1) // 2
    for r0 in range(0, tt, rc):
        acc = jnp.zeros((rc, w), F32) + bdw_ref[...]
        for k in range(B_KERNEL):
            off = HALO + r0 + k - pad_b
            acc = acc + wdw_ref[k:k + 1, :] * gext[off:off + rc, :]
        mu = jnp.mean(acc, axis=-1, keepdims=True)
        cen = acc - mu
        var = jnp.mean(cen * cen, axis=-1, keepdims=True)
        y = cen * lax.rsqrt(var + LN_EPS) * lng_ref[...] + lnb_ref[...]
        ob_ref[0, r0:r0 + rc, :] = y * _sigmoid(y)

        accc = jnp.zeros((rc, w), F32)
        for k in range(C_KERNEL):
            off = HALO + r0 + k - pad_c
            accc = accc + wc_ref[k:k + 1, :] * pext[off:off + rc, :]
        oc_ref[0, r0:r0 + rc, :] = uc_ref[0, r0:r0 + rc, 0:w] * accc


def _conv(ub, uc, w_dw, b_dw, ln_g, ln_b, w_c, tt=256, rc=64):
    b, t, _ = ub.shape
    tt = min(tt, t)
    w = GROUP_W
    hb = tt // HALO
    last = t // HALO - 1
    cur = lambda bi, i: (bi, i, 0)
    prev = lambda bi, i: (bi, jnp.maximum(i * hb - 1, 0), 0)
    nxt = lambda bi, i: (bi, jnp.minimum((i + 1) * hb, last), 0)
    const = lambda bi, i: (0, 0)
    return pl.pallas_call(
        functools.partial(_conv_kernel, tt=tt, rc=rc),
        grid=(b, t // tt),
        in_specs=[pl.BlockSpec((1, tt, 2 * w), cur),
                  pl.BlockSpec((1, HALO, 2 * w), prev),
                  pl.BlockSpec((1, HALO, 2 * w), nxt),
                  pl.BlockSpec((1, tt, 3 * w), cur),
                  pl.BlockSpec((1, HALO, 3 * w), prev),
                  pl.BlockSpec((1, HALO, 3 * w), nxt),
                  pl.BlockSpec((B_KERNEL, w), const),
                  pl.BlockSpec((1, w), const), pl.BlockSpec((1, w), const),
                  pl.BlockSpec((1, w), const),
                  pl.BlockSpec((C_KERNEL, w), const)],
        out_specs=[pl.BlockSpec((1, tt, w), cur), pl.BlockSpec((1, tt, w), cur)],
        out_shape=[jax.ShapeDtypeStruct((b, t, w), F32), jax.ShapeDtypeStruct((b, t, w), F32)],
        scratch_shapes=[pltpu.VMEM((tt + 2 * HALO, w), F32), pltpu.VMEM((tt + 2 * HALO, w), F32)],
        compiler_params=_cparams(("parallel", "parallel")),
        name="conv_bc",
    )(ub, ub, ub, uc, uc, uc, w_dw, b_dw, ln_g, ln_b, w_c)


def _attn_d_kernel(q_ref, k_ref, v_ref, b_ref, o_ref, *, t):
    g = pl.program_id(1)
    start = pl.multiple_of(jnp.clip((g - 1) * D_TQ, 0, t - D_TK), D_TQ)
    q = q_ref[0].astype(F32)
    k = k_ref[0, pl.ds(start, D_TK), :]
    v = v_ref[0, pl.ds(start, D_TK), :]
    lane_h = lax.broadcasted_iota(jnp.int32, (D_TQ, GROUP_W), 1) // D_HDIM
    out = jnp.zeros((D_TQ, GROUP_W), F32)
    for h in range(D_HEADS):
        qm = jnp.where(lane_h == h, q, 0.0).astype(BF16)
        s = lax.dot_general(qm, k, (((1,), (1,)), ((), ())),
                            preferred_element_type=F32) + b_ref[0, h]
        m = jnp.max(s, axis=-1, keepdims=True)
        p = jnp.exp2(s - m)
        l = jnp.sum(p, axis=-1, keepdims=True)
        o = jnp.dot(p.astype(BF16), v, preferred_element_type=F32)
        out = jnp.where(lane_h == h, o / l, out)
    o_ref[0] = out


def _d_bias_index():
    rows = 3 * D_KROWS
    groups = rows // D_QROWS
    ridx = np.zeros((3, D_TQ, D_TK), np.int32)
    cidx = np.zeros((3, D_TQ, D_TK), np.int32)
    valid = np.zeros((3, D_TQ, D_TK), bool)
    for var, g in enumerate((0, groups // 2, groups - 1)):
        start_row = int(np.clip(D_QROWS * (g - 1), 0, rows - D_KROWS))
        qr = D_QROWS * g + np.arange(D_TQ) // GRID_W
        qc = np.arange(D_TQ) % GRID_W
        kr = start_row + np.arange(D_TK) // GRID_W
        kc = np.arange(D_TK) % GRID_W
        rs = np.clip(qr - WIN_ROWS // 2, 0, rows - WIN_ROWS)
        cs = np.clip(qc - WIN_COLS // 2, 0, GRID_W - WIN_COLS)
        ok_r = (kr[None, :] >= rs[:, None]) & (kr[None, :] < rs[:, None] + WIN_ROWS)
        ok_c = (kc[None, :] >= cs[:, None]) & (kc[None, :] < cs[:, None] + WIN_COLS)
        valid[var] = ok_r & ok_c
        ridx[var] = np.clip(kr[None, :] - qr[:, None] + WIN_ROWS - 1, 0, 2 * WIN_ROWS - 2)
        cidx[var] = np.clip(kc[None, :] - qc[:, None] + WIN_COLS - 1, 0, 2 * WIN_COLS - 2)
    return ridx, cidx, valid


def _d_bias(rpb):
    ridx, cidx, valid = _d_bias_index()
    tab = rpb[:, ridx, cidx] * LOG2E
    tab = jnp.where(valid[None], tab, NEG_BIG)
    return tab.transpose(1, 0, 2, 3)


def _attn_d(qd, kd, vd, bias):
    b, t, w = qd.shape
    assert t % D_TQ == 0 and t >= D_TK
    groups = t // D_TQ
    variant = lambda bi, g: (jnp.where(g == 0, 0, jnp.where(g == groups - 1, 2, 1)), 0, 0, 0)
    return pl.pallas_call(
        functools.partial(_attn_d_kernel, t=t),
        grid=(b, groups),
        in_specs=[pl.BlockSpec((1, D_TQ, w), lambda bi, g: (bi, g, 0)),
                  pl.BlockSpec((1, t, w), lambda bi, g: (bi, 0, 0)),
                  pl.BlockSpec((1, t, w), lambda bi, g: (bi, 0, 0)),
                  pl.BlockSpec((1, D_HEADS, D_TQ, D_TK), variant)],
        out_specs=pl.BlockSpec((1, D_TQ, w), lambda bi, g: (bi, g, 0)),
        out_shape=jax.ShapeDtypeStruct((b, t, w), F32),
        compiler_params=_cparams(("parallel", "arbitrary")),
        name="attn_d",
    )(qd, kd, vd, bias)


def _outproj_kernel(x_ref, oa_ref, ob_ref, oc_ref, od_ref, w_ref, g1_ref, sh_ref, sc_ref,
                    n2_ref, wr_ref, br_ref, xo_ref, h2_ref, lg_ref):
    w = GROUP_W
    mix = jnp.dot(oa_ref[...].astype(BF16), w_ref[0:w, :], preferred_element_type=F32)
    mix += jnp.dot(ob_ref[...].astype(BF16), w_ref[w:2 * w, :], preferred_element_type=F32)
    mix += jnp.dot(oc_ref[...].astype(BF16), w_ref[2 * w:3 * w, :], preferred_element_type=F32)
    mix += jnp.dot(od_ref[...].astype(BF16), w_ref[3 * w:4 * w, :], preferred_element_type=F32)
    xn = x_ref[...] + g1_ref[...] * mix
    xo_ref[...] = xn
    ms = jnp.mean(xn * xn, axis=-1, keepdims=True)
    h2 = xn * lax.rsqrt(ms + NORM_EPS) * n2_ref[...]
    h2 = h2 * (1.0 + sc_ref[...]) + sh_ref[...]
    h_hi = h2.astype(BF16)
    h_lo = (h2 - h_hi.astype(F32)).astype(BF16)
    wr = wr_ref[...]
    w_hi = wr.astype(BF16)
    w_lo = (wr - w_hi.astype(F32)).astype(BF16)
    lg = (jnp.dot(h_hi, w_hi, preferred_element_type=F32)
          + jnp.dot(h_hi, w_lo, preferred_element_type=F32)
          + jnp.dot(h_lo, w_hi, preferred_element_type=F32))
    lg_ref[...] = lg + br_ref[...]
    h2_ref[...] = h_hi


def _outproj(x2, t, oa, ob, oc, od, w_out, mod4, norm_g, w_router, b_router, tm=512):
    n, d = x2.shape
    tm = min(tm, t)
    tpb = t // tm
    w = GROUP_W
    row = lambda i: (i, 0)
    const = lambda i: (0, 0)
    mod_spec = lambda k: pl.BlockSpec((None, None, 1, d), lambda i: (i // tpb, k, 0, 0))
    return pl.pallas_call(
        _outproj_kernel,
        grid=(n // tm,),
        in_specs=[pl.BlockSpec((tm, d), row),
                  pl.BlockSpec((tm, w), row), pl.BlockSpec((tm, w), row),
                  pl.BlockSpec((tm, w), row), pl.BlockSpec((tm, w), row),
                  pl.BlockSpec((d, d), const),
                  mod_spec(2), mod_spec(3), mod_spec(4),
                  pl.BlockSpec((1, d), const),
                  pl.BlockSpec((d, N_EXPERTS), const),
                  pl.BlockSpec((1, N_EXPERTS), const)],
        out_specs=[pl.BlockSpec((tm, d), row), pl.BlockSpec((tm, d), row),
                   pl.BlockSpec((tm, N_EXPERTS), row)],
        out_shape=[jax.ShapeDtypeStruct((n, d), F32), jax.ShapeDtypeStruct((n, d), BF16),
                   jax.ShapeDtypeStruct((n, N_EXPERTS), F32)],
        compiler_params=_cparams(("parallel",)),
        name="outproj",
    )(x2, oa, ob, oc, od, w_out, mod4, mod4, mod4, norm_g, w_router, b_router)


def _moe_kernel(be_ref, nu_ref, x_ref, g_ref, w1_ref, b1_ref, w2_ref, b2_ref, y_ref):
    i = pl.program_id(0)

    @pl.when(i < nu_ref[0])
    def _compute():
        h = jnp.dot(x_ref[...], w1_ref[0].astype(BF16), preferred_element_type=F32) + b1_ref[0]
        glu = jnp.minimum(h[:, :D_FF], SWIGLU_LIMIT)
        lin = jnp.clip(h[:, D_FF:], -SWIGLU_LIMIT, SWIGLU_LIMIT)
        act = glu * _sigmoid(SWIGLU_ALPHA * glu) * (lin + 1.0)
        y = jnp.dot(act.astype(BF16), w2_ref[0].astype(BF16),
                    preferred_element_type=F32) + b2_ref[0]
        y_ref[...] = (y * g_ref[...]).astype(y_ref.dtype)

    @pl.when(i >= nu_ref[0])
    def _unused():
        y_ref[...] = jnp.zeros(y_ref.shape, y_ref.dtype)


def _moe(xs, row_gate, block_expert, n_used, w1, b1, w2, b2, tm):
    n_rows, d = xs.shape
    e, _, f2 = w1.shape
    n_blocks = n_rows // tm
    grid_spec = pltpu.PrefetchScalarGridSpec(
        num_scalar_prefetch=2,
        grid=(n_blocks,),
        in_specs=[pl.BlockSpec((tm, d), lambda i, be, nu: (i, 0)),
                  pl.BlockSpec((tm, 1), lambda i, be, nu: (i, 0)),
                  pl.BlockSpec((1, d, f2), lambda i, be, nu: (be[i], 0, 0)),
                  pl.BlockSpec((1, 1, f2), lambda i, be, nu: (be[i], 0, 0)),
                  pl.BlockSpec((1, f2 // 2, d), lambda i, be, nu: (be[i], 0, 0)),
                  pl.BlockSpec((1, 1, d), lambda i, be, nu: (be[i], 0, 0))],
        out_specs=pl.BlockSpec((tm, d), lambda i, be, nu: (i, 0)),
    )
    return pl.pallas_call(
        _moe_kernel,
        grid_spec=grid_spec,
        out_shape=jax.ShapeDtypeStruct((n_rows, d), BF16),
        compiler_params=_cparams(("arbitrary",)),
        name="moe_experts",
    )(block_expert, n_used, xs, row_gate.reshape(n_rows, 1), w1,
      b1.reshape(e, 1, f2), w2, b2.reshape(e, 1, d))


def _route(logits, tm):
    n = logits.shape[0]
    top_val, top_idx = lax.top_k(logits, TOP_K)
    gates = jax.nn.softmax(top_val, axis=-1)
    nk = n * TOP_K
    e_flat = top_idx.reshape(nk).astype(jnp.int32)
    onehot = (e_flat[:, None] == jnp.arange(N_EXPERTS, dtype=jnp.int32)[None, :]).astype(jnp.int32)
    csum = jnp.cumsum(onehot, axis=0)
    pos_in_e = jnp.sum((csum - 1) * onehot, axis=1)
    counts = csum[-1]
    padded = (counts + tm - 1) // tm * tm
    pend = jnp.cumsum(padded)
    pstart = pend - padded
    dest = (pstart[e_flat] + pos_in_e).astype(jnp.int32)
    n_blocks = nk // tm + N_EXPERTS
    n_rows = n_blocks * tm
    row_tok = jnp.zeros((n_rows,), jnp.int32).at[dest].set(
        jnp.arange(nk, dtype=jnp.int32) // TOP_K)
    row_gate = jnp.zeros((n_rows,), F32).at[dest].set(gates.reshape(nk))
    block_expert = jnp.minimum(
        jnp.searchsorted(pend, jnp.arange(n_blocks, dtype=jnp.int32) * tm, side='right'),
        N_EXPERTS - 1).astype(jnp.int32)
    n_used = (pend[-1] // tm).astype(jnp.int32).reshape(1)
    return dest, row_tok, row_gate, block_expert, n_used


def _combine_kernel(x_ref, y_ref, g2_ref, o_ref):
    y = y_ref[0].astype(F32)
    for k in range(1, TOP_K):
        y = y + y_ref[k].astype(F32)
    o_ref[...] = x_ref[...] + g2_ref[...] * y


def _combine(x2, t, yk, mod4, tm=512):
    n, d = x2.shape
    tm = min(tm, t)
    tpb = t // tm
    return pl.pallas_call(
        _combine_kernel,
        grid=(n // tm,),
        in_specs=[pl.BlockSpec((tm, d), lambda i: (i, 0)),
                  pl.BlockSpec((TOP_K, tm, d), lambda i: (0, i, 0)),
                  pl.BlockSpec((None, None, 1, d), lambda i: (i // tpb, 5, 0, 0))],
        out_specs=pl.BlockSpec((tm, d), lambda i: (i, 0)),
        out_shape=jax.ShapeDtypeStruct((n, d), F32),
        compiler_params=_cparams(("parallel",)),
        name="combine",
    )(x2, yk, mod4)


def _tile_gain(g, reps):
    return jnp.tile(g.astype(F32), reps).reshape(1, -1)


def _trunk(x, mods, p, moe_tm=MOE_TM):
    b, t, d = x.shape
    n = b * t
    depth = p["norm1"].shape[0]
    x2 = x.reshape(n, d)
    w = GROUP_W
    for l in range(depth):
        lambda_init = 0.8 - 0.6 * math.exp(-0.3 * l)
        mod4 = mods[l].reshape(b, N_MOD, 1, d)
        qa, ka, va, ub, uc, qd, kd, vd = _inproj(
            x2, t, p["norm1"][l].reshape(1, d), mod4, p["w_in"][l].astype(BF16),
            _tile_gain(p["a_q_gain"][l], w // A_QK), _tile_gain(p["a_k_gain"][l], w // A_QK),
            _tile_gain(p["d_q_gain"][l], w // D_HDIM), _tile_gain(p["d_k_gain"][l], w // D_HDIM))

        kt = ka.reshape(b, t, w).transpose(0, 2, 1)
        vh = va.reshape(b, t, A_HEADS, A_VDIM).transpose(0, 2, 1, 3)
        v_aug = jnp.concatenate(
            [vh, jnp.ones((b, A_HEADS, t, 1), BF16), jnp.zeros((b, A_HEADS, t, A_VDIM - 1), BF16)],
            axis=-1)
        sub_gain = jnp.concatenate([p["a_sub_gain"][l].astype(F32),
                                    jnp.zeros((A_VDIM,), F32)]).reshape(1, 2 * A_VDIM)
        oa = _attn_a(qa.reshape(b, t, w), kt, v_aug, p["a_lambda"][l], sub_gain, lambda_init)

        ob, oc = _conv(ub.reshape(b, t, 2 * w), uc.reshape(b, t, 3 * w), p["b_dw"][l],
                       p["b_dw_bias"][l].reshape(1, w), p["b_ln_g"][l].reshape(1, w),
                       p["b_ln_b"][l].reshape(1, w), p["c_conv"][l])

        od = _attn_d(qd.reshape(b, t, w), kd.reshape(b, t, w), vd.reshape(b, t, w),
                     _d_bias(p["d_rpb"][l]))

        x2, h2, logits = _outproj(
            x2, t, oa.reshape(n, w), ob.reshape(n, w), oc.reshape(n, w), od.reshape(n, w),
            p["w_out"][l].astype(BF16), mod4, p["norm2"][l].reshape(1, d),
            p["w_router"][l], p["b_router"][l].reshape(1, N_EXPERTS))

        dest, row_tok, row_gate, block_expert, n_used = _route(logits, moe_tm)
        xs = h2[row_tok]
        y = _moe(xs, row_gate, block_expert, n_used, p["w1"][l], p["b1"][l], p["w2"][l],
                 p["b2"][l], moe_tm)
        yk = y[dest.reshape(n, TOP_K).T]
        x2 = _combine(x2, t, yk, mod4)
    return x2.reshape(b, t, d)


def kernel(x_prompt, x_sample, c_prompt, c_sample, norm1, norm2, w_ada, b_ada, w_in, w_out, a_q_gain, a_k_gain, a_lambda, a_sub_gain, b_dw, b_dw_bias, b_ln_g, b_ln_b, c_conv, d_q_gain, d_k_gain, d_rpb, w_router, b_router, w1, b1, w2, b2):
    p = dict(norm1=norm1, norm2=norm2, w_in=w_in, w_out=w_out, a_q_gain=a_q_gain,
             a_k_gain=a_k_gain, a_lambda=a_lambda, a_sub_gain=a_sub_gain, b_dw=b_dw,
             b_dw_bias=b_dw_bias, b_ln_g=b_ln_g, b_ln_b=b_ln_b, c_conv=c_conv,
             d_q_gain=d_q_gain, d_k_gain=d_k_gain, d_rpb=d_rpb, w_router=w_router,
             b_router=b_router, w1=w1, b1=b1, w2=w2, b2=b2)
    bp = c_prompt.shape[0]
    bs = c_sample.shape[0]
    rows = -(-(bp + bs) // 8) * 8
    c_all = jnp.concatenate([c_prompt, c_sample,
                             jnp.zeros((rows - bp - bs, c_prompt.shape[1]), F32)], axis=0)
    mods = _ada(c_all, w_ada, b_ada)
    y_prompt = _trunk(x_prompt, mods[:, :bp], p)
    y_sample = _trunk(x_sample, mods[:, bp:bp + bs], p)
    return (y_prompt, y_sample)
```

```python
import functools
import math

import numpy as np
import jax
import jax.numpy as jnp
from jax import lax
from jax.experimental import pallas as pl
from jax.experimental.pallas import tpu as pltpu

F32 = jnp.float32
BF16 = jnp.bfloat16

D_MODEL = 1024
GROUP_W = 256
A_HEADS = 4
A_VDIM = 64
A_QK = 32
B_KERNEL = 31
C_KERNEL = 3
D_HEADS = 4
D_HDIM = 64
GRID_W = 64
WIN_ROWS = 8
WIN_COLS = 16
IN_COLS = 11 * GROUP_W
N_EXPERTS = 32
TOP_K = 4
D_FF = 1024
SWIGLU_ALPHA = 1.702
SWIGLU_LIMIT = 7.0
N_MOD = 6
NORM_EPS = 1e-6
LN_EPS = 1e-5

LOG2E = 1.4426950408889634
NEG_BIG = -1e30
LANES = 128
A_AUG = 16
A_KDIM = A_QK + A_AUG
A_POS_SPLIT = 256
HALO = 16
D_QROWS = 4
D_KROWS = 12
D_TQ = D_QROWS * GRID_W
D_TK = D_KROWS * GRID_W
MOE_TM = 512
RANK_CHUNK = 128
VMEM_LIMIT = 56 * 1024 * 1024


def _cparams(sems, vmem=VMEM_LIMIT):
    return pltpu.CompilerParams(dimension_semantics=sems, vmem_limit_bytes=vmem)


def _sigmoid(x):
    return 1.0 / (1.0 + jnp.exp(-x))


def _ada_kernel(c_ref, w_ref, b_ref, o_ref):
    c = c_ref[...]
    s = c * _sigmoid(c)
    o_ref[0] = jnp.dot(s.astype(BF16), w_ref[0].astype(BF16),
                       preferred_element_type=F32) + b_ref[0]


def _ada(c_all, w_ada, b_ada):
    depth, d, n6 = w_ada.shape
    rows = c_all.shape[0]
    tn = 1536
    return pl.pallas_call(
        _ada_kernel,
        grid=(depth, n6 // tn),
        in_specs=[pl.BlockSpec((rows, d), lambda l, j: (0, 0)),
                  pl.BlockSpec((1, d, tn), lambda l, j: (l, 0, j)),
                  pl.BlockSpec((1, 1, tn), lambda l, j: (l, 0, j))],
        out_specs=pl.BlockSpec((1, rows, tn), lambda l, j: (l, 0, j)),
        out_shape=jax.ShapeDtypeStruct((depth, rows, n6), F32),
        compiler_params=_cparams(("parallel", "parallel")),
        name="ada_mod",
    )(c_all, w_ada, b_ada.reshape(depth, 1, n6))


def _inproj_kernel(x_ref, g_ref, sh_ref, sc_ref, w_ref, g32_ref, g64_ref,
                   aq_ref, ak_ref, dq_ref, dk_ref,
                   qa_ref, ka_ref, va_ref, ub_ref, uc_ref, qd_ref, kd_ref, vd_ref):
    x = x_ref[...]
    ms = jnp.mean(x * x, axis=-1, keepdims=True)
    h = x * lax.rsqrt(ms + NORM_EPS) * g_ref[...]
    h = h * (1.0 + sc_ref[...]) + sh_ref[...]
    hb = h.astype(BF16)

    def proj(lo, hi):
        return jnp.dot(hb, w_ref[:, lo:hi], preferred_element_type=F32)

    def group_norm(u, gmat_ref, gain_ref, post):
        u2 = u * u
        u2_hi = u2.astype(BF16)
        u2_lo = (u2 - u2_hi.astype(F32)).astype(BF16)
        gmat = gmat_ref[...]
        gms = (jnp.dot(u2_hi, gmat, preferred_element_type=F32)
               + jnp.dot(u2_lo, gmat, preferred_element_type=F32))
        return u * lax.rsqrt(gms + NORM_EPS) * (gain_ref[...] * post)

    w = GROUP_W
    a_scale = (A_QK ** -0.5) * LOG2E
    d_scale = (D_HDIM ** -0.5) * LOG2E
    qa_ref[...] = group_norm(proj(0, w), g32_ref, aq_ref, a_scale).astype(BF16)
    ka_ref[...] = group_norm(proj(w, 2 * w), g32_ref, ak_ref, 1.0).astype(BF16)
    va_ref[...] = proj(2 * w, 3 * w).astype(BF16)
    ub_ref[...] = proj(3 * w, 5 * w)
    uc_ref[...] = proj(5 * w, 8 * w)
    qd_ref[...] = group_norm(proj(8 * w, 9 * w), g64_ref, dq_ref, d_scale).astype(BF16)
    kd_ref[...] = group_norm(proj(9 * w, 10 * w), g64_ref, dk_ref, 1.0).astype(BF16)
    vd_ref[...] = proj(10 * w, 11 * w).astype(BF16)


def _group_mean_matrix(group):
    idx = np.arange(GROUP_W) // group
    return jnp.asarray((idx[:, None] == idx[None, :]).astype(np.float32) / group, dtype=BF16)


def _inproj(x2, t, norm_g, mod4, w_in, aq, ak, dq, dk, tm=512):
    n, d = x2.shape
    tm = min(tm, t)
    tpb = t // tm
    row = lambda i: (i, 0)
    const = lambda i: (0, 0)
    w = GROUP_W
    mod_spec = lambda k: pl.BlockSpec((None, None, 1, d), lambda i: (i // tpb, k, 0, 0))
    out_cols = [(w, BF16), (w, BF16), (w, BF16), (2 * w, F32), (3 * w, F32),
                (w, BF16), (w, BF16), (w, BF16)]
    return pl.pallas_call(
        _inproj_kernel,
        grid=(n // tm,),
        in_specs=[pl.BlockSpec((tm, d), row),
                  pl.BlockSpec((1, d), const),
                  mod_spec(0), mod_spec(1),
                  pl.BlockSpec((d, IN_COLS), const),
                  pl.BlockSpec((w, w), const), pl.BlockSpec((w, w), const),
                  pl.BlockSpec((1, w), const), pl.BlockSpec((1, w), const),
                  pl.BlockSpec((1, w), const), pl.BlockSpec((1, w), const)],
        out_specs=[pl.BlockSpec((tm, c), row) for c, _ in out_cols],
        out_shape=[jax.ShapeDtypeStruct((n, c), dt) for c, dt in out_cols],
        compiler_params=_cparams(("parallel",)),
        name="inproj",
    )(x2, norm_g, mod4, mod4, w_in, _group_mean_matrix(A_QK), _group_mean_matrix(D_HDIM),
      aq, ak, dq, dk)


def _bf16_pieces(x, n=3):
    out = []
    for _ in range(n):
        piece = float(np.asarray(x, dtype=BF16).astype(np.float32))
        out.append(piece)
        x = x - piece
    return out


def _alibi_slope(h):
    slope = 2.0 ** (-8.0 * (h + 1) / A_HEADS)
    assert math.log2(slope) == round(math.log2(slope)), "slopes must be powers of two"
    return slope


def _a_operands(qa, ka, va, tk):
    b, t, w = qa.shape
    n_maps = 2 * A_HEADS
    pos = np.arange(t) % tk
    lo = (pos % A_POS_SPLIT).astype(np.float32)
    hi = (pos - pos % A_POS_SPLIT).astype(np.float32)
    assert hi.max() / A_POS_SPLIT <= A_POS_SPLIT
    pieces = _bf16_pieces(LOG2E)
    ones = np.ones((t,), np.float32)
    k_rows = np.zeros((A_AUG, t), np.float32)
    k_rows[0:3] = lo
    k_rows[3:6] = hi
    for r, piece in enumerate(pieces):
        k_rows[6 + r] = piece * ones
        k_rows[9 + r] = piece * ones
    q_cols = np.zeros((A_HEADS, 2, t, A_AUG), np.float32)
    for h in range(A_HEADS):
        slope = _alibi_slope(h)
        for var, sign in enumerate((1.0, -1.0)):
            for r, piece in enumerate(pieces):
                q_cols[h, var, :, r] = -sign * slope * piece
                q_cols[h, var, :, 3 + r] = -sign * slope * piece
                q_cols[h, var, :, 6 + r] = sign * slope * lo
                q_cols[h, var, :, 9 + r] = sign * slope * hi
    q_cols = np.repeat(q_cols, 2, axis=0)
    kt = ka.reshape(b, t, n_maps, A_QK).transpose(0, 2, 3, 1)
    kt_aug = jnp.concatenate(
        [kt, jnp.broadcast_to(jnp.asarray(k_rows, BF16), (b, n_maps, A_AUG, t))], axis=2)
    qh = qa.reshape(b, t, n_maps, A_QK).transpose(0, 2, 1, 3)
    q_aug = jnp.concatenate(
        [jnp.broadcast_to(qh[:, :, None], (b, n_maps, 2, t, A_QK)),
         jnp.broadcast_to(jnp.asarray(q_cols, BF16), (b, n_maps, 2, t, A_AUG))], axis=-1)
    vh = va.reshape(b, t, A_HEADS, A_VDIM).transpose(0, 2, 1, 3)
    v_aug = jnp.concatenate(
        [vh, jnp.ones((b, A_HEADS, t, 1), BF16), jnp.zeros((b, A_HEADS, t, A_VDIM - 1), BF16)],
        axis=-1)
    return q_aug, kt_aug, v_aug


def _attn_a_kernel(q_ref, kt_ref, v_ref, lam_ref, sg_ref, o_ref, m_sc, acc_sc,
                   *, tq, tk, lambda_init):
    i = pl.program_id(1)
    j = pl.program_id(2)
    nj = pl.num_programs(2)
    n_maps = 2 * A_HEADS
    jd = (i * tq) // tk

    @pl.when(j == 0)
    def _init():
        m_sc[...] = jnp.full(m_sc.shape, NEG_BIG, F32)
        acc_sc[...] = jnp.zeros(acc_sc.shape, F32)

    def softmax_step(hc, s, rho):
        m_prev = m_sc[hc]
        m_new = jnp.maximum(m_prev, jnp.max(s, axis=-1, keepdims=True) + rho)
        alpha = jnp.exp2(m_prev - m_new)
        mu = m_new - rho
        p = jnp.exp2(s - jnp.concatenate([mu] * (tk // LANES), axis=1))
        acc_sc[hc] = alpha * acc_sc[hc] + jnp.dot(p.astype(BF16), v_ref[0, hc // 2],
                                                  preferred_element_type=F32)
        m_sc[hc] = m_new

    @pl.when(j != jd)
    def _off_diagonal():
        var = jnp.where(j > jd, 0, 1)
        tile_dist = (jnp.abs(j - jd) * tk).astype(F32)
        for hc in range(n_maps):
            s = jnp.dot(q_ref[0, hc, var], kt_ref[0, hc], preferred_element_type=F32)
            softmax_step(hc, s, tile_dist * (-_alibi_slope(hc // 2) * LOG2E))

    @pl.when(j == jd)
    def _diagonal():
        for hc in range(n_maps):
            kt = kt_ref[0, hc]
            s = jnp.minimum(jnp.dot(q_ref[0, hc, 0], kt, preferred_element_type=F32),
                            jnp.dot(q_ref[0, hc, 1], kt, preferred_element_type=F32))
            softmax_step(hc, s, 0.0)

    @pl.when(j == nj - 1)
    def _finish():
        lv = lam_ref[...]
        lam = (jnp.exp(jnp.sum(lv[0:1] * lv[1:2], axis=-1, keepdims=True))
               - jnp.exp(jnp.sum(lv[2:3] * lv[3:4], axis=-1, keepdims=True)) + lambda_init)
        lane = lax.broadcasted_iota(jnp.int32, (tq, 2 * A_VDIM), 1)
        heads = []
        for h in range(A_HEADS):
            a1 = acc_sc[2 * h]
            a2 = acc_sc[2 * h + 1]
            l1 = jnp.sum(jnp.where(lane == A_VDIM, a1, 0.0), axis=-1, keepdims=True)
            l2 = jnp.sum(jnp.where(lane == A_VDIM, a2, 0.0), axis=-1, keepdims=True)
            o = jnp.where(lane < A_VDIM, a1 / l1 - lam * (a2 / l2), 0.0)
            ms = jnp.sum(o * o, axis=-1, keepdims=True) * (1.0 / A_VDIM)
            heads.append(o * lax.rsqrt(ms + NORM_EPS) * (sg_ref[...] * (1.0 - lambda_init)))
        o_ref[0, :, 0:128] = heads[0] + pltpu.roll(heads[1], A_VDIM, 1)
        o_ref[0, :, 128:256] = heads[2] + pltpu.roll(heads[3], A_VDIM, 1)


def _attn_a(qa, ka, va, lam, sub_gain, lambda_init, tq=256, tk=1024):
    b, t, w = qa.shape
    tq = min(tq, t)
    tk = min(tk, t)
    assert tk % tq == 0 and tk % A_POS_SPLIT == 0 and t % tk == 0
    n_maps = 2 * A_HEADS
    q_aug, kt_aug, v_aug = _a_operands(qa, ka, va, tk)
    return pl.pallas_call(
        functools.partial(_attn_a_kernel, tq=tq, tk=tk, lambda_init=lambda_init),
        grid=(b, t // tq, t // tk),
        in_specs=[pl.BlockSpec((1, n_maps, 2, tq, A_KDIM), lambda bi, i, j: (bi, 0, 0, i, 0)),
                  pl.BlockSpec((1, n_maps, A_KDIM, tk), lambda bi, i, j: (bi, 0, 0, j)),
                  pl.BlockSpec((1, A_HEADS, tk, 2 * A_VDIM), lambda bi, i, j: (bi, 0, j, 0)),
                  pl.BlockSpec((4, A_QK), lambda bi, i, j: (0, 0)),
                  pl.BlockSpec((1, 2 * A_VDIM), lambda bi, i, j: (0, 0))],
        out_specs=pl.BlockSpec((1, tq, w), lambda bi, i, j: (bi, i, 0)),
        out_shape=jax.ShapeDtypeStruct((b, t, w), F32),
        scratch_shapes=[pltpu.VMEM((n_maps, tq, LANES), F32),
                        pltpu.VMEM((n_maps, tq, 2 * A_VDIM), F32)],
        compiler_params=_cparams(("parallel", "parallel", "arbitrary")),
        name="attn_a",
    )(q_aug, kt_aug, v_aug, lam, sub_gain)


def _conv_kernel(ub_ref, ubp_ref, ubn_ref, uc_ref, ucp_ref, ucn_ref,
                 wdw_ref, bdw_ref, lng_ref, lnb_ref, wc_ref,
                 ob_ref, oc_ref, gext, pext, *, tt, rc):
    i = pl.program_id(1)
    n = pl.num_programs(1)
    has_prev = jnp.where(i > 0, 1.0, 0.0)
    has_next = jnp.where(i < n - 1, 1.0, 0.0)
    w = GROUP_W

    def glu(u):
        return u[:, :w] * _sigmoid(u[:, w:])

    def gated(u):
        return u[:, w:2 * w] * u[:, 2 * w:]

    gext[0:HALO, :] = glu(ubp_ref[0]) * has_prev
    gext[HALO:HALO + tt, :] = glu(ub_ref[0])
    gext[HALO + tt:2 * HALO + tt, :] = glu(ubn_ref[0]) * has_next
    pext[0:HALO, :] = gated(ucp_ref[0]) * has_prev
    pext[HALO:HALO + tt, :] = gated(uc_ref[0])
    pext[HALO + tt:2 * HALO + tt, :] = gated(ucn_ref[0]) * has_next

    pad_b = (B_KERNEL - 1) // 2
    pad_c = (C_KERNEL - 1) // 2
    for r0 in range(0, tt, rc):
        acc = jnp.zeros((rc, w), F32) + bdw_ref[...]
        for k in range(B_KERNEL):
            off = HALO + r0 + k - pad_b
            acc = acc + wdw_ref[k:k + 1, :] * gext[off:off + rc, :]
        mu = jnp.mean(acc, axis=-1, keepdims=True)
        cen = acc - mu
        var = jnp.mean(cen * cen, axis=-1, keepdims=True)
        y = cen * lax.rsqrt(var + LN_EPS) * lng_ref[...] + lnb_ref[...]
        ob_ref[0, r0:r0 + rc, :] = y * _sigmoid(y)

        accc = jnp.zeros((rc, w), F32)
        for k in range(C_KERNEL):
            off = HALO + r0 + k - pad_c
            accc = accc + wc_ref[k:k + 1, :] * pext[off:off + rc, :]
        oc_ref[0, r0:r0 + rc, :] = uc_ref[0, r0:r0 + rc, 0:w] * accc


def _conv(ub, uc, w_dw, b_dw, ln_g, ln_b, w_c, tt=256, rc=64):
    b, t, _ = ub.shape
    tt = min(tt, t)
    w = GROUP_W
    hb = tt // HALO
    last = t // HALO - 1
    cur = lambda bi, i: (bi, i, 0)
    prev = lambda bi, i: (bi, jnp.maximum(i * hb - 1, 0), 0)
    nxt = lambda bi, i: (bi, jnp.minimum((i + 1) * hb, last), 0)
    const = lambda bi, i: (0, 0)
    return pl.pallas_call(
        functools.partial(_conv_kernel, tt=tt, rc=rc),
        grid=(b, t // tt),
        in_specs=[pl.BlockSpec((1, tt, 2 * w), cur),
                  pl.BlockSpec((1, HALO, 2 * w), prev),
                  pl.BlockSpec((1, HALO, 2 * w), nxt),
                  pl.BlockSpec((1, tt, 3 * w), cur),
                  pl.BlockSpec((1, HALO, 3 * w), prev),
                  pl.BlockSpec((1, HALO, 3 * w), nxt),
                  pl.BlockSpec((B_KERNEL, w), const),
                  pl.BlockSpec((1, w), const), pl.BlockSpec((1, w), const),
                  pl.BlockSpec((1, w), const),
                  pl.BlockSpec((C_KERNEL, w), const)],
        out_specs=[pl.BlockSpec((1, tt, w), cur), pl.BlockSpec((1, tt, w), cur)],
        out_shape=[jax.ShapeDtypeStruct((b, t, w), F32), jax.ShapeDtypeStruct((b, t, w), F32)],
        scratch_shapes=[pltpu.VMEM((tt + 2 * HALO, w), F32), pltpu.VMEM((tt + 2 * HALO, w), F32)],
        compiler_params=_cparams(("parallel", "parallel")),
        name="conv_bc",
    )(ub, ub, ub, uc, uc, uc, w_dw, b_dw, ln_g, ln_b, w_c)


def _attn_d_kernel(q_ref, k_ref, v_ref, b_ref, o_ref, *, t):
    g = pl.program_id(1)
    start = pl.multiple_of(jnp.clip((g - 1) * D_TQ, 0, t - D_TK), D_TQ)
    q = q_ref[0].astype(F32)
    k = k_ref[0, pl.ds(start, D_TK), :]
    v = v_ref[0, pl.ds(start, D_TK), :]
    lane_h = lax.broadcasted_iota(jnp.int32, (D_TQ, GROUP_W), 1) // D_HDIM
    out = jnp.zeros((D_TQ, GROUP_W), F32)
    for h in range(D_HEADS):
        qm = jnp.where(lane_h == h, q, 0.0).astype(BF16)
        s = lax.dot_general(qm, k, (((1,), (1,)), ((), ())),
                            preferred_element_type=F32) + b_ref[0, h]
        m = jnp.max(s, axis=-1, keepdims=True)
        p = jnp.exp2(s - m)
        l = jnp.sum(p, axis=-1, keepdims=True)
        o = jnp.dot(p.astype(BF16), v, preferred_element_type=F32)
        out = jnp.where(lane_h == h, o / l, out)
    o_ref[0] = out


def _d_tables():
    qc = np.arange(GRID_W)
    kc = np.arange(GRID_W)
    cs = np.clip(qc - WIN_COLS // 2, 0, GRID_W - WIN_COLS)
    ok_c = (kc[None, :] >= cs[:, None]) & (kc[None, :] < cs[:, None] + WIN_COLS)
    crel = kc[None, :] - qc[:, None] + WIN_COLS - 1
    n_c = 2 * WIN_COLS - 1
    col_sel = (crel[None] == np.arange(n_c)[:, None, None]).astype(np.float32)
    rows = 3 * D_KROWS
    groups = rows // D_QROWS
    n_r = 2 * WIN_ROWS - 1
    row_sel = np.zeros((3, D_QROWS, D_KROWS, n_r), np.float32)
    ok_r = np.zeros((3, D_QROWS, D_KROWS), bool)
    for var, g in enumerate((0, groups // 2, groups - 1)):
        start_row = int(np.clip(D_QROWS * (g - 1), 0, rows - D_KROWS))
        for a in range(D_QROWS):
            qr = D_QROWS * g + a
            rs = int(np.clip(qr - WIN_ROWS // 2, 0, rows - WIN_ROWS))
            for kb in range(D_KROWS):
                kr = start_row + kb
                if rs <= kr < rs + WIN_ROWS:
                    ok_r[var, a, kb] = True
                    row_sel[var, a, kb, kr - qr + WIN_ROWS - 1] = 1.0
    valid = ok_r[:, :, None, :, None] & ok_c[None, None, :, None, :]
    return col_sel, row_sel, valid.reshape(3, D_TQ, D_TK)


def _d_bias(rpb):
    col_sel, row_sel, valid = _d_tables()
    hp = lax.Precision.HIGHEST
    tab = jnp.einsum('hrc,cqk->hrqk', rpb.astype(F32), col_sel, precision=hp)
    bias = jnp.einsum('vabr,hrqk->vhaqbk', row_sel, tab, precision=hp)
    bias = bias.reshape(3, D_HEADS, D_TQ, D_TK) * LOG2E
    return jnp.where(valid[:, None], bias, NEG_BIG)


def _attn_d(qd, kd, vd, bias):
    b, t, w = qd.shape
    assert t % D_TQ == 0 and t >= D_TK
    groups = t // D_TQ
    variant = lambda bi, g: (jnp.where(g == 0, 0, jnp.where(g == groups - 1, 2, 1)), 0, 0, 0)
    return pl.pallas_call(
        functools.partial(_attn_d_kernel, t=t),
        grid=(b, groups),
        in_specs=[pl.BlockSpec((1, D_TQ, w), lambda bi, g: (bi, g, 0)),
                  pl.BlockSpec((1, t, w), lambda bi, g: (bi, 0, 0)),
                  pl.BlockSpec((1, t, w), lambda bi, g: (bi, 0, 0)),
                  pl.BlockSpec((1, D_HEADS, D_TQ, D_TK), variant)],
        out_specs=pl.BlockSpec((1, D_TQ, w), lambda bi, g: (bi, g, 0)),
        out_shape=jax.ShapeDtypeStruct((b, t, w), F32),
        compiler_params=_cparams(("parallel", "arbitrary")),
        name="attn_d",
    )(qd, kd, vd, bias)


def _outproj_kernel(x_ref, oa_ref, ob_ref, oc_ref, od_ref, w_ref, g1_ref, sh_ref, sc_ref,
                    n2_ref, wr_ref, br_ref, xo_ref, h2_ref, lg_ref):
    w = GROUP_W
    mix = jnp.dot(oa_ref[...].astype(BF16), w_ref[0:w, :], preferred_element_type=F32)
    mix += jnp.dot(ob_ref[...].astype(BF16), w_ref[w:2 * w, :], preferred_element_type=F32)
    mix += jnp.dot(oc_ref[...].astype(BF16), w_ref[2 * w:3 * w, :], preferred_element_type=F32)
    mix += jnp.dot(od_ref[...].astype(BF16), w_ref[3 * w:4 * w, :], preferred_element_type=F32)
    xn = x_ref[...] + g1_ref[...] * mix
    xo_ref[...] = xn
    ms = jnp.mean(xn * xn, axis=-1, keepdims=True)
    h2 = xn * lax.rsqrt(ms + NORM_EPS) * n2_ref[...]
    h2 = h2 * (1.0 + sc_ref[...]) + sh_ref[...]
    h_hi = h2.astype(BF16)
    h_lo = (h2 - h_hi.astype(F32)).astype(BF16)
    wr = wr_ref[...]
    w_hi = wr.astype(BF16)
    w_lo = (wr - w_hi.astype(F32)).astype(BF16)
    lg = (jnp.dot(h_hi, w_hi, preferred_element_type=F32)
          + jnp.dot(h_hi, w_lo, preferred_element_type=F32)
          + jnp.dot(h_lo, w_hi, preferred_element_type=F32))
    lg_ref[...] = lg + br_ref[...]
    h2_ref[...] = h_hi


def _outproj(x2, t, oa, ob, oc, od, w_out, mod4, norm_g, w_router, b_router, tm=512):
    n, d = x2.shape
    tm = min(tm, t)
    tpb = t // tm
    w = GROUP_W
    row = lambda i: (i, 0)
    const = lambda i: (0, 0)
    mod_spec = lambda k: pl.BlockSpec((None, None, 1, d), lambda i: (i // tpb, k, 0, 0))
    return pl.pallas_call(
        _outproj_kernel,
        grid=(n // tm,),
        in_specs=[pl.BlockSpec((tm, d), row),
                  pl.BlockSpec((tm, w), row), pl.BlockSpec((tm, w), row),
                  pl.BlockSpec((tm, w), row), pl.BlockSpec((tm, w), row),
                  pl.BlockSpec((d, d), const),
                  mod_spec(2), mod_spec(3), mod_spec(4),
                  pl.BlockSpec((1, d), const),
                  pl.BlockSpec((d, N_EXPERTS), const),
                  pl.BlockSpec((1, N_EXPERTS), const)],
        out_specs=[pl.BlockSpec((tm, d), row), pl.BlockSpec((tm, d), row),
                   pl.BlockSpec((tm, N_EXPERTS), row)],
        out_shape=[jax.ShapeDtypeStruct((n, d), F32), jax.ShapeDtypeStruct((n, d), BF16),
                   jax.ShapeDtypeStruct((n, N_EXPERTS), F32)],
        compiler_params=_cparams(("parallel",)),
        name="outproj",
    )(x2, oa, ob, oc, od, w_out, mod4, mod4, mod4, norm_g, w_router, b_router)


def _moe_kernel(be_ref, nu_ref, x_ref, w1_ref, b1_ref, w2_ref, b2_ref, y_ref):
    i = pl.program_id(0)

    @pl.when(i < nu_ref[0])
    def _compute():
        h = jnp.dot(x_ref[...], w1_ref[0].astype(BF16), preferred_element_type=F32) + b1_ref[0]
        glu = jnp.minimum(h[:, :D_FF], SWIGLU_LIMIT)
        lin = jnp.clip(h[:, D_FF:], -SWIGLU_LIMIT, SWIGLU_LIMIT)
        act = glu * _sigmoid(SWIGLU_ALPHA * glu) * (lin + 1.0)
        y = jnp.dot(act.astype(BF16), w2_ref[0].astype(BF16),
                    preferred_element_type=F32) + b2_ref[0]
        y_ref[...] = y.astype(y_ref.dtype)

    @pl.when(i >= nu_ref[0])
    def _unused():
        y_ref[...] = jnp.zeros(y_ref.shape, y_ref.dtype)


def _moe(xs, block_expert, n_used, w1, b1, w2, b2, tm):
    n_rows, d = xs.shape
    e, _, f2 = w1.shape
    n_blocks = n_rows // tm
    grid_spec = pltpu.PrefetchScalarGridSpec(
        num_scalar_prefetch=2,
        grid=(n_blocks,),
        in_specs=[pl.BlockSpec((tm, d), lambda i, be, nu: (i, 0)),
                  pl.BlockSpec((1, d, f2), lambda i, be, nu: (be[i], 0, 0)),
                  pl.BlockSpec((1, 1, f2), lambda i, be, nu: (be[i], 0, 0)),
                  pl.BlockSpec((1, f2 // 2, d), lambda i, be, nu: (be[i], 0, 0)),
                  pl.BlockSpec((1, 1, d), lambda i, be, nu: (be[i], 0, 0))],
        out_specs=pl.BlockSpec((tm, d), lambda i, be, nu: (i, 0)),
    )
    return pl.pallas_call(
        _moe_kernel,
        grid_spec=grid_spec,
        out_shape=jax.ShapeDtypeStruct((n_rows, d), BF16),
        compiler_params=_cparams(("arbitrary",)),
        name="moe_experts",
    )(block_expert, n_used, xs, w1, b1.reshape(e, 1, f2), w2, b2.reshape(e, 1, d))


def _route(logits, tm):
    n = logits.shape[0]
    top_val, top_idx = lax.top_k(logits, TOP_K)
    gates = jax.nn.softmax(top_val, axis=-1)
    nk = n * TOP_K
    e_flat = top_idx.reshape(nk).astype(jnp.int32)
    onehot = (e_flat[:, None] == jnp.arange(N_EXPERTS, dtype=jnp.int32)[None, :])
    oh = onehot.astype(BF16).reshape(nk // RANK_CHUNK, RANK_CHUNK, N_EXPERTS)
    tri = jnp.asarray(np.tril(np.ones((RANK_CHUNK, RANK_CHUNK), np.float32)), BF16)
    within = jnp.einsum('ij,cjk->cik', tri, oh, preferred_element_type=F32)
    tot = within[:, -1, :]
    offs = jnp.cumsum(tot, axis=0) - tot
    rank = (within + offs[:, None, :] - 1.0).reshape(nk, N_EXPERTS)
    pos_in_e = jnp.sum(jnp.where(onehot, rank, 0.0), axis=1).astype(jnp.int32)
    counts = (offs[-1] + tot[-1]).astype(jnp.int32)
    padded = (counts + tm - 1) // tm * tm
    pend = jnp.cumsum(padded)
    pstart = pend - padded
    dest = (pstart[e_flat] + pos_in_e).astype(jnp.int32)
    n_blocks = nk // tm + N_EXPERTS
    n_rows = n_blocks * tm
    row_tok = jnp.zeros((n_rows,), jnp.int32).at[dest].set(
        jnp.arange(nk, dtype=jnp.int32) // TOP_K)
    block_expert = jnp.minimum(
        jnp.searchsorted(pend, jnp.arange(n_blocks, dtype=jnp.int32) * tm, side='right'),
        N_EXPERTS - 1).astype(jnp.int32)
    n_used = (pend[-1] // tm).astype(jnp.int32).reshape(1)
    return dest, row_tok, gates, block_expert, n_used


def _combine_kernel(x_ref, y_ref, gt_ref, g2_ref, o_ref):
    reps = x_ref.shape[1] // LANES
    acc = None
    for k in range(TOP_K):
        gate = jnp.concatenate([gt_ref[k]] * reps, axis=1)
        term = gate * y_ref[k].astype(F32)
        acc = term if acc is None else acc + term
    o_ref[...] = x_ref[...] + g2_ref[...] * acc


def _combine(x2, t, yk, gates_b, mod4, tm=512):
    n, d = x2.shape
    tm = min(tm, t)
    tpb = t // tm
    return pl.pallas_call(
        _combine_kernel,
        grid=(n // tm,),
        in_specs=[pl.BlockSpec((tm, d), lambda i: (i, 0)),
                  pl.BlockSpec((TOP_K, tm, d), lambda i: (0, i, 0)),
                  pl.BlockSpec((TOP_K, tm, LANES), lambda i: (0, i, 0)),
                  pl.BlockSpec((None, None, 1, d), lambda i: (i // tpb, 5, 0, 0))],
        out_specs=pl.BlockSpec((tm, d), lambda i: (i, 0)),
        out_shape=jax.ShapeDtypeStruct((n, d), F32),
        compiler_params=_cparams(("parallel",)),
        name="combine",
    )(x2, yk, gates_b, mod4)


def _tile_gain(g, reps):
    return jnp.tile(g.astype(F32), reps).reshape(1, -1)


def _trunk(x, mods, p, moe_tm=MOE_TM):
    b, t, d = x.shape
    n = b * t
    depth = p["norm1"].shape[0]
    x2 = x.reshape(n, d)
    w = GROUP_W
    for l in range(depth):
        lambda_init = 0.8 - 0.6 * math.exp(-0.3 * l)
        mod4 = mods[l].reshape(b, N_MOD, 1, d)
        qa, ka, va, ub, uc, qd, kd, vd = _inproj(
            x2, t, p["norm1"][l].reshape(1, d), mod4, p["w_in"][l].astype(BF16),
            _tile_gain(p["a_q_gain"][l], w // A_QK), _tile_gain(p["a_k_gain"][l], w // A_QK),
            _tile_gain(p["d_q_gain"][l], w // D_HDIM), _tile_gain(p["d_k_gain"][l], w // D_HDIM))

        sub_gain = jnp.concatenate([p["a_sub_gain"][l].astype(F32),
                                    jnp.zeros((A_VDIM,), F32)]).reshape(1, 2 * A_VDIM)
        oa = _attn_a(qa.reshape(b, t, w), ka.reshape(b, t, w), va.reshape(b, t, w),
                     p["a_lambda"][l], sub_gain, lambda_init)

        ob, oc = _conv(ub.reshape(b, t, 2 * w), uc.reshape(b, t, 3 * w), p["b_dw"][l],
                       p["b_dw_bias"][l].reshape(1, w), p["b_ln_g"][l].reshape(1, w),
                       p["b_ln_b"][l].reshape(1, w), p["c_conv"][l])

        od = _attn_d(qd.reshape(b, t, w), kd.reshape(b, t, w), vd.reshape(b, t, w),
                     _d_bias(p["d_rpb"][l]))

        x2, h2, logits = _outproj(
            x2, t, oa.reshape(n, w), ob.reshape(n, w), oc.reshape(n, w), od.reshape(n, w),
            p["w_out"][l].astype(BF16), mod4, p["norm2"][l].reshape(1, d),
            p["w_router"][l], p["b_router"][l].reshape(1, N_EXPERTS))

        dest, row_tok, gates, block_expert, n_used = _route(logits, moe_tm)
        xs = h2[row_tok]
        y = _moe(xs, block_expert, n_used, p["w1"][l], p["b1"][l], p["w2"][l], p["b2"][l],
                 moe_tm)
        yk = y[dest.reshape(n, TOP_K).T]
        gates_b = jnp.broadcast_to(gates.T[:, :, None], (TOP_K, n, LANES))
        x2 = _combine(x2, t, yk, gates_b, mod4)
    return x2.reshape(b, t, d)


def kernel(x_prompt, x_sample, c_prompt, c_sample, norm1, norm2, w_ada, b_ada, w_in, w_out, a_q_gain, a_k_gain, a_lambda, a_sub_gain, b_dw, b_dw_bias, b_ln_g, b_ln_b, c_conv, d_q_gain, d_k_gain, d_rpb, w_router, b_router, w1, b1, w2, b2):
    p = dict(norm1=norm1, norm2=norm2, w_in=w_in, w_out=w_out, a_q_gain=a_q_gain,
             a_k_gain=a_k_gain, a_lambda=a_lambda, a_sub_gain=a_sub_gain, b_dw=b_dw,
             b_dw_bias=b_dw_bias, b_ln_g=b_ln_g, b_ln_b=b_ln_b, c_conv=c_conv,
             d_q_gain=d_q_gain, d_k_gain=d_k_gain, d_rpb=d_rpb, w_router=w_router,
             b_router=b_router, w1=w1, b1=b1, w2=w2, b2=b2)
    bp = c_prompt.shape[0]
    bs = c_sample.shape[0]
    rows = -(-(bp + bs) // 8) * 8
    c_all = jnp.concatenate([c_prompt, c_sample,
                             jnp.zeros((rows - bp - bs, c_prompt.shape[1]), F32)], axis=0)
    mods = _ada(c_all, w_ada, b_ada)
    y_prompt = _trunk(x_prompt, mods[:, :bp], p)
    y_sample = _trunk(x_sample, mods[:, bp:bp + bs], p)
    return (y_prompt, y_sample)
```

```python
import functools
import math

import numpy as np
import jax
import jax.numpy as jnp
from jax import lax
from jax.experimental import pallas as pl
from jax.experimental.pallas import tpu as pltpu

F32 = jnp.float32
BF16 = jnp.bfloat16

D_MODEL = 1024
GROUP_W = 256
A_HEADS = 4
A_VDIM = 64
A_QK = 32
B_KERNEL = 31
C_KERNEL = 3
D_HEADS = 4
D_HDIM = 64
GRID_W = 64
WIN_ROWS = 8
WIN_COLS = 16
IN_COLS = 11 * GROUP_W
N_EXPERTS = 32
TOP_K = 4
D_FF = 1024
SWIGLU_ALPHA = 1.702
SWIGLU_LIMIT = 7.0
N_MOD = 6
NORM_EPS = 1e-6
LN_EPS = 1e-5

LOG2E = 1.4426950408889634
NEG_BIG = -1e30
LANES = 128
A_AUG = 16
A_KDIM = A_QK + A_AUG
A_POS_SPLIT = 256
HALO = 16
D_QROWS = 4
D_KROWS = 12
D_TQ = D_QROWS * GRID_W
D_TK = D_KROWS * GRID_W
MOE_TM = 512
RANK_CHUNK = 128
VMEM_LIMIT = 56 * 1024 * 1024


def _cparams(sems, vmem=VMEM_LIMIT):
    return pltpu.CompilerParams(dimension_semantics=sems, vmem_limit_bytes=vmem)


def _sigmoid(x):
    return 1.0 / (1.0 + jnp.exp(-x))


def _ada_kernel(c_ref, w_ref, b_ref, o_ref):
    c = c_ref[...]
    s = c * _sigmoid(c)
    o_ref[0] = jnp.dot(s.astype(BF16), w_ref[0].astype(BF16),
                       preferred_element_type=F32) + b_ref[0]


def _ada(c_all, w_ada, b_ada):
    depth, d, n6 = w_ada.shape
    rows = c_all.shape[0]
    tn = 1536
    return pl.pallas_call(
        _ada_kernel,
        grid=(depth, n6 // tn),
        in_specs=[pl.BlockSpec((rows, d), lambda l, j: (0, 0)),
                  pl.BlockSpec((1, d, tn), lambda l, j: (l, 0, j)),
                  pl.BlockSpec((1, 1, tn), lambda l, j: (l, 0, j))],
        out_specs=pl.BlockSpec((1, rows, tn), lambda l, j: (l, 0, j)),
        out_shape=jax.ShapeDtypeStruct((depth, rows, n6), F32),
        compiler_params=_cparams(("parallel", "parallel")),
        name="ada_mod",
    )(c_all, w_ada, b_ada.reshape(depth, 1, n6))


def _inproj_kernel(x_ref, g_ref, sh_ref, sc_ref, w_ref, g32_ref, g64_ref,
                   aq_ref, ak_ref, dq_ref, dk_ref, selq_ref, selv_ref, vone_ref,
                   qa_ref, kt_ref, va_ref, ub_ref, uc_ref, qd_ref, kd_ref, vd_ref):
    x = x_ref[...]
    ms = jnp.mean(x * x, axis=-1, keepdims=True)
    h = x * lax.rsqrt(ms + NORM_EPS) * g_ref[...]
    h = h * (1.0 + sc_ref[...]) + sh_ref[...]
    hb = h.astype(BF16)

    def proj(lo, hi):
        return jnp.dot(hb, w_ref[:, lo:hi], preferred_element_type=F32)

    def group_norm(u, gmat_ref, gain_ref, post):
        u2 = u * u
        u2_hi = u2.astype(BF16)
        u2_lo = (u2 - u2_hi.astype(F32)).astype(BF16)
        gmat = gmat_ref[...]
        gms = (jnp.dot(u2_hi, gmat, preferred_element_type=F32)
               + jnp.dot(u2_lo, gmat, preferred_element_type=F32))
        return u * lax.rsqrt(gms + NORM_EPS) * (gain_ref[...] * post)

    w = GROUP_W
    a_scale = (A_QK ** -0.5) * LOG2E
    d_scale = (D_HDIM ** -0.5) * LOG2E
    qa = group_norm(proj(0, w), g32_ref, aq_ref, a_scale).astype(BF16)
    qa_ref[...] = jnp.dot(qa, selq_ref[...], preferred_element_type=F32).astype(BF16)
    kt_ref[...] = group_norm(proj(w, 2 * w), g32_ref, ak_ref, 1.0).T.astype(BF16)
    va = proj(2 * w, 3 * w).astype(BF16)
    va_ref[...] = (jnp.dot(va, selv_ref[...], preferred_element_type=F32)
                   + vone_ref[...]).astype(BF16)
    ub_ref[...] = proj(3 * w, 5 * w)
    uc_ref[...] = proj(5 * w, 8 * w)
    qd_ref[...] = group_norm(proj(8 * w, 9 * w), g64_ref, dq_ref, d_scale).astype(BF16)
    kd_ref[...] = group_norm(proj(9 * w, 10 * w), g64_ref, dk_ref, 1.0).astype(BF16)
    vd_ref[...] = proj(10 * w, 11 * w).astype(BF16)


def _group_mean_matrix(group):
    idx = np.arange(GROUP_W) // group
    return jnp.asarray((idx[:, None] == idx[None, :]).astype(np.float32) / group, dtype=BF16)


def _inproj(x2, t, norm_g, mod4, w_in, aq, ak, dq, dk, tm=512):
    n, d = x2.shape
    tm = min(tm, t)
    tpb = t // tm
    row = lambda i: (i, 0)
    const = lambda i: (0, 0)
    w = GROUP_W
    mod_spec = lambda k: pl.BlockSpec((None, None, 1, d), lambda i: (i // tpb, k, 0, 0))
    n_maps = 2 * A_HEADS
    chan = np.arange(w)
    selq = np.zeros((w, n_maps * LANES), np.float32)
    selq[chan, LANES * (chan // A_QK) + chan % A_QK] = 1.0
    selv = np.zeros((w, A_HEADS * LANES), np.float32)
    selv[chan, LANES * (chan // A_VDIM) + chan % A_VDIM] = 1.0
    vone = np.zeros((1, A_HEADS * LANES), np.float32)
    vone[0, LANES * np.arange(A_HEADS) + A_VDIM] = 1.0
    out_cols = [(n_maps * LANES, BF16), None, (A_HEADS * LANES, BF16), (2 * w, F32),
                (3 * w, F32), (w, BF16), (w, BF16), (w, BF16)]
    out_specs = [pl.BlockSpec((w, tm), lambda i: (0, i)) if c is None
                 else pl.BlockSpec((tm, c[0]), row) for c in out_cols]
    out_shape = [jax.ShapeDtypeStruct((w, n), BF16) if c is None
                 else jax.ShapeDtypeStruct((n, c[0]), c[1]) for c in out_cols]
    return pl.pallas_call(
        _inproj_kernel,
        grid=(n // tm,),
        in_specs=[pl.BlockSpec((tm, d), row),
                  pl.BlockSpec((1, d), const),
                  mod_spec(0), mod_spec(1),
                  pl.BlockSpec((d, IN_COLS), const),
                  pl.BlockSpec((w, w), const), pl.BlockSpec((w, w), const),
                  pl.BlockSpec((1, w), const), pl.BlockSpec((1, w), const),
                  pl.BlockSpec((1, w), const), pl.BlockSpec((1, w), const),
                  pl.BlockSpec((w, n_maps * LANES), const),
                  pl.BlockSpec((w, A_HEADS * LANES), const),
                  pl.BlockSpec((1, A_HEADS * LANES), const)],
        out_specs=out_specs,
        out_shape=out_shape,
        compiler_params=_cparams(("parallel",)),
        name="inproj",
    )(x2, norm_g, mod4, mod4, w_in, _group_mean_matrix(A_QK), _group_mean_matrix(D_HDIM),
      aq, ak, dq, dk, jnp.asarray(selq, BF16), jnp.asarray(selv, BF16), jnp.asarray(vone, F32))


def _bf16_pieces(x, n=3):
    out = []
    for _ in range(n):
        piece = float(np.asarray(x, dtype=BF16).astype(np.float32))
        out.append(piece)
        x = x - piece
    return out


def _alibi_slope(h):
    slope = 2.0 ** (-8.0 * (h + 1) / A_HEADS)
    assert math.log2(slope) == round(math.log2(slope)), "slopes must be powers of two"
    return slope


def _a_constants(tk):
    pos = np.arange(tk)
    lo = (pos % A_POS_SPLIT).astype(np.float32)
    hi = (pos - pos % A_POS_SPLIT).astype(np.float32)
    assert hi.max() / A_POS_SPLIT <= A_POS_SPLIT
    pieces = _bf16_pieces(LOG2E)
    k_rows = np.zeros((A_AUG, tk), np.float32)
    k_rows[0:3] = lo
    k_rows[3:6] = hi
    for r, piece in enumerate(pieces):
        k_rows[6 + r] = piece
        k_rows[9 + r] = piece
    q_cols = np.zeros((A_HEADS, 2, tk, A_KDIM), np.float32)
    for h in range(A_HEADS):
        slope = _alibi_slope(h)
        for var, sign in enumerate((1.0, -1.0)):
            for r, piece in enumerate(pieces):
                q_cols[h, var, :, A_QK + r] = -sign * slope * piece
                q_cols[h, var, :, A_QK + 3 + r] = -sign * slope * piece
                q_cols[h, var, :, A_QK + 6 + r] = sign * slope * lo
                q_cols[h, var, :, A_QK + 9 + r] = sign * slope * hi
    return jnp.asarray(q_cols, BF16), jnp.asarray(k_rows, BF16)


def _attn_a_kernel(q_ref, kt_ref, v_ref, qc_ref, kr_ref, lam_ref, sg_ref, o_ref,
                   qa_sc, kt_sc, m_sc, acc_sc, *, tq, tk):
    i = pl.program_id(1)
    j = pl.program_id(2)
    nj = pl.num_programs(2)
    n_maps = 2 * A_HEADS
    jd = (i * tq) // tk

    @pl.when(j == 0)
    def _init():
        for hc in range(n_maps):
            base = q_ref[0, :, LANES * hc:LANES * hc + A_KDIM].astype(F32)
            for var in range(2):
                qa_sc[hc, var] = (base + qc_ref[hc // 2, var].astype(F32)).astype(BF16)
            kt_sc[hc, A_QK:A_KDIM, :] = kr_ref[...]
        m_sc[...] = jnp.full(m_sc.shape, NEG_BIG, F32)
        acc_sc[...] = jnp.zeros(acc_sc.shape, F32)

    for hc in range(n_maps):
        kt_sc[hc, 0:A_QK, :] = kt_ref[A_QK * hc:A_QK * (hc + 1), :]

    def softmax_step(hc, s, rho):
        m_prev = m_sc[hc]
        m_new = jnp.maximum(m_prev, jnp.max(s, axis=-1, keepdims=True) + rho)
        alpha = jnp.exp2(m_prev - m_new)
        mu = m_new - rho
        p = jnp.exp2(s - jnp.concatenate([mu] * (tk // LANES), axis=1))
        h = hc // 2
        acc_sc[hc] = alpha * acc_sc[hc] + jnp.dot(p.astype(BF16),
                                                  v_ref[0, :, LANES * h:LANES * (h + 1)],
                                                  preferred_element_type=F32)
        m_sc[hc] = m_new

    @pl.when(j != jd)
    def _off_diagonal():
        var = jnp.where(j > jd, 0, 1)
        tile_dist = (jnp.abs(j - jd) * tk).astype(F32)
        for hc in range(n_maps):
            s = jnp.dot(qa_sc[hc, var], kt_sc[hc], preferred_element_type=F32)
            softmax_step(hc, s, tile_dist * (-_alibi_slope(hc // 2) * LOG2E))

    @pl.when(j == jd)
    def _diagonal():
        for hc in range(n_maps):
            kt = kt_sc[hc]
            s = jnp.minimum(jnp.dot(qa_sc[hc, 0], kt, preferred_element_type=F32),
                            jnp.dot(qa_sc[hc, 1], kt, preferred_element_type=F32))
            softmax_step(hc, s, 0.0)

    @pl.when(j == nj - 1)
    def _finish():
        lv = lam_ref[...]
        lambda_init = lv[4:5, 0:1]
        lam = (jnp.exp(jnp.sum(lv[0:1] * lv[1:2], axis=-1, keepdims=True))
               - jnp.exp(jnp.sum(lv[2:3] * lv[3:4], axis=-1, keepdims=True)) + lambda_init)
        lane = lax.broadcasted_iota(jnp.int32, (tq, 2 * A_VDIM), 1)
        heads = []
        for h in range(A_HEADS):
            a1 = acc_sc[2 * h]
            a2 = acc_sc[2 * h + 1]
            l1 = jnp.sum(jnp.where(lane == A_VDIM, a1, 0.0), axis=-1, keepdims=True)
            l2 = jnp.sum(jnp.where(lane == A_VDIM, a2, 0.0), axis=-1, keepdims=True)
            o = jnp.where(lane < A_VDIM, a1 / l1 - lam * (a2 / l2), 0.0)
            ms = jnp.sum(o * o, axis=-1, keepdims=True) * (1.0 / A_VDIM)
            heads.append(o * lax.rsqrt(ms + NORM_EPS) * (sg_ref[...] * (1.0 - lambda_init)))
        o_ref[0, :, 0:128] = heads[0] + pltpu.roll(heads[1], A_VDIM, 1)
        o_ref[0, :, 128:256] = heads[2] + pltpu.roll(heads[3], A_VDIM, 1)


def _attn_a(q_sel, kt, v_aug, lam, sub_gain, lambda_init, tq=512, tk=1024):
    b, t, _ = q_sel.shape
    lam = jnp.concatenate([lam.astype(F32), jnp.full((1, A_QK), lambda_init, F32)], axis=0)
    w = GROUP_W
    tq = min(tq, t)
    tk = min(tk, t)
    assert tk % tq == 0 and tk % A_POS_SPLIT == 0 and t % tk == 0
    n_maps = 2 * A_HEADS
    nkt = t // tk
    qpt = tk // tq
    q_cols, k_rows = _a_constants(tk)
    return pl.pallas_call(
        functools.partial(_attn_a_kernel, tq=tq, tk=tk),
        grid=(b, t // tq, nkt),
        in_specs=[pl.BlockSpec((1, tq, n_maps * LANES), lambda bi, i, j: (bi, i, 0)),
                  pl.BlockSpec((w, tk), lambda bi, i, j: (0, bi * nkt + j)),
                  pl.BlockSpec((1, tk, A_HEADS * LANES), lambda bi, i, j: (bi, j, 0)),
                  pl.BlockSpec((A_HEADS, 2, tq, A_KDIM), lambda bi, i, j: (0, 0, i % qpt, 0)),
                  pl.BlockSpec((A_AUG, tk), lambda bi, i, j: (0, 0)),
                  pl.BlockSpec((5, A_QK), lambda bi, i, j: (0, 0)),
                  pl.BlockSpec((1, 2 * A_VDIM), lambda bi, i, j: (0, 0))],
        out_specs=pl.BlockSpec((1, tq, w), lambda bi, i, j: (bi, i, 0)),
        out_shape=jax.ShapeDtypeStruct((b, t, w), F32),
        scratch_shapes=[pltpu.VMEM((n_maps, 2, tq, A_KDIM), BF16),
                        pltpu.VMEM((n_maps, A_KDIM, tk), BF16),
                        pltpu.VMEM((n_maps, tq, LANES), F32),
                        pltpu.VMEM((n_maps, tq, 2 * A_VDIM), F32)],
        compiler_params=_cparams(("parallel", "parallel", "arbitrary")),
        name="attn_a",
    )(q_sel, kt, v_aug, q_cols, k_rows, lam, sub_gain)


def _conv_kernel(ub_ref, ubp_ref, ubn_ref, uc_ref, ucp_ref, ucn_ref,
                 wdw_ref, bdw_ref, lng_ref, lnb_ref, wc_ref,
                 ob_ref, oc_ref, gext, pext, *, tt, rc):
    i = pl.program_id(1)
    n = pl.num_programs(1)
    has_prev = jnp.where(i > 0, 1.0, 0.0)
    has_next = jnp.where(i < n - 1, 1.0, 0.0)
    w = GROUP_W

    def glu(u):
        return u[:, :w] * _sigmoid(u[:, w:])

    def gated(u):
        return u[:, w:2 * w] * u[:, 2 * w:]

    gext[0:HALO, :] = glu(ubp_ref[0]) * has_prev
    gext[HALO:HALO + tt, :] = glu(ub_ref[0])
    gext[HALO + tt:2 * HALO + tt, :] = glu(ubn_ref[0]) * has_next
    pext[0:HALO, :] = gated(ucp_ref[0]) * has_prev
    pext[HALO:HALO + tt, :] = gated(uc_ref[0])
    pext[HALO + tt:2 * HALO + tt, :] = gated(ucn_ref[0]) * has_next

    pad_b = (B_KERNEL - 1) // 2
    pad_c = (C_KERNEL - 1) // 2
    for r0 in range(0, tt, rc):
        acc = jnp.zeros((rc, w), F32) + bdw_ref[...]
        for k in range(B_KERNEL):
            off = HALO + r0 + k - pad_b
            acc = acc + wdw_ref[k:k + 1, :] * gext[off:off + rc, :]
        mu = jnp.mean(acc, axis=-1, keepdims=True)
        cen = acc - mu
        var = jnp.mean(cen * cen, axis=-1, keepdims=True)
        y = cen * lax.rsqrt(var + LN_EPS) * lng_ref[...] + lnb_ref[...]
        ob_ref[0, r0:r0 + rc, :] = y * _sigmoid(y)

        accc = jnp.zeros((rc, w), F32)
        for k in range(C_KERNEL):
            off = HALO + r0 + k - pad_c
            accc = accc + wc_ref[k:k + 1, :] * pext[off:off + rc, :]
        oc_ref[0, r0:r0 + rc, :] = uc_ref[0, r0:r0 + rc, 0:w] * accc


def _conv(ub, uc, w_dw, b_dw, ln_g, ln_b, w_c, tt=256, rc=64):
    b, t, _ = ub.shape
    tt = min(tt, t)
    w = GROUP_W
    hb = tt // HALO
    last = t // HALO - 1
    cur = lambda bi, i: (bi, i, 0)
    prev = lambda bi, i: (bi, jnp.maximum(i * hb - 1, 0), 0)
    nxt = lambda bi, i: (bi, jnp.minimum((i + 1) * hb, last), 0)
    const = lambda bi, i: (0, 0)
    return pl.pallas_call(
        functools.partial(_conv_kernel, tt=tt, rc=rc),
        grid=(b, t // tt),
        in_specs=[pl.BlockSpec((1, tt, 2 * w), cur),
                  pl.BlockSpec((1, HALO, 2 * w), prev),
                  pl.BlockSpec((1, HALO, 2 * w), nxt),
                  pl.BlockSpec((1, tt, 3 * w), cur),
                  pl.BlockSpec((1, HALO, 3 * w), prev),
                  pl.BlockSpec((1, HALO, 3 * w), nxt),
                  pl.BlockSpec((B_KERNEL, w), const),
                  pl.BlockSpec((1, w), const), pl.BlockSpec((1, w), const),
                  pl.BlockSpec((1, w), const),
                  pl.BlockSpec((C_KERNEL, w), const)],
        out_specs=[pl.BlockSpec((1, tt, w), cur), pl.BlockSpec((1, tt, w), cur)],
        out_shape=[jax.ShapeDtypeStruct((b, t, w), F32), jax.ShapeDtypeStruct((b, t, w), F32)],
        scratch_shapes=[pltpu.VMEM((tt + 2 * HALO, w), F32), pltpu.VMEM((tt + 2 * HALO, w), F32)],
        compiler_params=_cparams(("parallel", "parallel")),
        name="conv_bc",
    )(ub, ub, ub, uc, uc, uc, w_dw, b_dw, ln_g, ln_b, w_c)


def _attn_d_kernel(q_ref, k_ref, v_ref, b_ref, o_ref, *, t):
    g = pl.program_id(1)
    start = pl.multiple_of(jnp.clip((g - 1) * D_TQ, 0, t - D_TK), D_TQ)
    q = q_ref[0].astype(F32)
    k = k_ref[0, pl.ds(start, D_TK), :]
    v = v_ref[0, pl.ds(start, D_TK), :]
    lane_h = lax.broadcasted_iota(jnp.int32, (D_TQ, GROUP_W), 1) // D_HDIM
    out = jnp.zeros((D_TQ, GROUP_W), F32)
    for h in range(D_HEADS):
        qm = jnp.where(lane_h == h, q, 0.0).astype(BF16)
        s = lax.dot_general(qm, k, (((1,), (1,)), ((), ())),
                            preferred_element_type=F32) + b_ref[0, h]
        m = jnp.max(s, axis=-1, keepdims=True)
        p = jnp.exp2(s - m)
        l = jnp.sum(p, axis=-1, keepdims=True)
        o = jnp.dot(p.astype(BF16), v, preferred_element_type=F32)
        out = jnp.where(lane_h == h, o / l, out)
    o_ref[0] = out


def _d_tables():
    qc = np.arange(GRID_W)
    kc = np.arange(GRID_W)
    cs = np.clip(qc - WIN_COLS // 2, 0, GRID_W - WIN_COLS)
    ok_c = (kc[None, :] >= cs[:, None]) & (kc[None, :] < cs[:, None] + WIN_COLS)
    crel = kc[None, :] - qc[:, None] + WIN_COLS - 1
    n_c = 2 * WIN_COLS - 1
    col_sel = (crel[None] == np.arange(n_c)[:, None, None]).astype(np.float32)
    rows = 3 * D_KROWS
    groups = rows // D_QROWS
    n_r = 2 * WIN_ROWS - 1
    row_sel = np.zeros((3, D_QROWS, D_KROWS, n_r), np.float32)
    ok_r = np.zeros((3, D_QROWS, D_KROWS), bool)
    for var, g in enumerate((0, groups // 2, groups - 1)):
        start_row = int(np.clip(D_QROWS * (g - 1), 0, rows - D_KROWS))
        for a in range(D_QROWS):
            qr = D_QROWS * g + a
            rs = int(np.clip(qr - WIN_ROWS // 2, 0, rows - WIN_ROWS))
            for kb in range(D_KROWS):
                kr = start_row + kb
                if rs <= kr < rs + WIN_ROWS:
                    ok_r[var, a, kb] = True
                    row_sel[var, a, kb, kr - qr + WIN_ROWS - 1] = 1.0
    valid = ok_r[:, :, None, :, None] & ok_c[None, None, :, None, :]
    return col_sel, row_sel, valid.reshape(3, D_TQ, D_TK)


def _d_bias(rpb):
    col_sel, row_sel, valid = _d_tables()
    hp = lax.Precision.HIGHEST
    tab = jnp.einsum('hrc,cqk->hrqk', rpb.astype(F32), col_sel, precision=hp)
    bias = jnp.einsum('vabr,hrqk->vhaqbk', row_sel, tab, precision=hp)
    bias = bias.reshape(3, D_HEADS, D_TQ, D_TK) * LOG2E
    return jnp.where(valid[:, None], bias, NEG_BIG)


def _attn_d(qd, kd, vd, bias):
    b, t, w = qd.shape
    assert t % D_TQ == 0 and t >= D_TK
    groups = t // D_TQ
    variant = lambda bi, g: (jnp.where(g == 0, 0, jnp.where(g == groups - 1, 2, 1)), 0, 0, 0)
    return pl.pallas_call(
        functools.partial(_attn_d_kernel, t=t),
        grid=(b, groups),
        in_specs=[pl.BlockSpec((1, D_TQ, w), lambda bi, g: (bi, g, 0)),
                  pl.BlockSpec((1, t, w), lambda bi, g: (bi, 0, 0)),
                  pl.BlockSpec((1, t, w), lambda bi, g: (bi, 0, 0)),
                  pl.BlockSpec((1, D_HEADS, D_TQ, D_TK), variant)],
        out_specs=pl.BlockSpec((1, D_TQ, w), lambda bi, g: (bi, g, 0)),
        out_shape=jax.ShapeDtypeStruct((b, t, w), F32),
        compiler_params=_cparams(("parallel", "arbitrary")),
        name="attn_d",
    )(qd, kd, vd, bias)


def _outproj_kernel(x_ref, oa_ref, ob_ref, oc_ref, od_ref, w_ref, g1_ref, sh_ref, sc_ref,
                    n2_ref, wr_ref, br_ref, xo_ref, h2_ref, lg_ref):
    w = GROUP_W
    mix = jnp.dot(oa_ref[...].astype(BF16), w_ref[0:w, :], preferred_element_type=F32)
    mix += jnp.dot(ob_ref[...].astype(BF16), w_ref[w:2 * w, :], preferred_element_type=F32)
    mix += jnp.dot(oc_ref[...].astype(BF16), w_ref[2 * w:3 * w, :], preferred_element_type=F32)
    mix += jnp.dot(od_ref[...].astype(BF16), w_ref[3 * w:4 * w, :], preferred_element_type=F32)
    xn = x_ref[...] + g1_ref[...] * mix
    xo_ref[...] = xn
    ms = jnp.mean(xn * xn, axis=-1, keepdims=True)
    h2 = xn * lax.rsqrt(ms + NORM_EPS) * n2_ref[...]
    h2 = h2 * (1.0 + sc_ref[...]) + sh_ref[...]
    h_hi = h2.astype(BF16)
    h_lo = (h2 - h_hi.astype(F32)).astype(BF16)
    wr = wr_ref[...]
    w_hi = wr.astype(BF16)
    w_lo = (wr - w_hi.astype(F32)).astype(BF16)
    lg = (jnp.dot(h_hi, w_hi, preferred_element_type=F32)
          + jnp.dot(h_hi, w_lo, preferred_element_type=F32)
          + jnp.dot(h_lo, w_hi, preferred_element_type=F32))
    lg_ref[...] = lg + br_ref[...]
    h2_ref[...] = h_hi


def _outproj(x2, t, oa, ob, oc, od, w_out, mod4, norm_g, w_router, b_router, tm=512):
    n, d = x2.shape
    tm = min(tm, t)
    tpb = t // tm
    w = GROUP_W
    row = lambda i: (i, 0)
    const = lambda i: (0, 0)
    mod_spec = lambda k: pl.BlockSpec((None, None, 1, d), lambda i: (i // tpb, k, 0, 0))
    return pl.pallas_call(
        _outproj_kernel,
        grid=(n // tm,),
        in_specs=[pl.BlockSpec((tm, d), row),
                  pl.BlockSpec((tm, w), row), pl.BlockSpec((tm, w), row),
                  pl.BlockSpec((tm, w), row), pl.BlockSpec((tm, w), row),
                  pl.BlockSpec((d, d), const),
                  mod_spec(2), mod_spec(3), mod_spec(4),
                  pl.BlockSpec((1, d), const),
                  pl.BlockSpec((d, N_EXPERTS), const),
                  pl.BlockSpec((1, N_EXPERTS), const)],
        out_specs=[pl.BlockSpec((tm, d), row), pl.BlockSpec((tm, d), row),
                   pl.BlockSpec((tm, N_EXPERTS), row)],
        out_shape=[jax.ShapeDtypeStruct((n, d), F32), jax.ShapeDtypeStruct((n, d), BF16),
                   jax.ShapeDtypeStruct((n, N_EXPERTS), F32)],
        compiler_params=_cparams(("parallel",)),
        name="outproj",
    )(x2, oa, ob, oc, od, w_out, mod4, mod4, mod4, norm_g, w_router, b_router)


def _moe_kernel(be_ref, nu_ref, x_ref, w1_ref, b1_ref, w2_ref, b2_ref, y_ref):
    i = pl.program_id(0)

    @pl.when(i < nu_ref[0])
    def _compute():
        h = jnp.dot(x_ref[...], w1_ref[0].astype(BF16), preferred_element_type=F32) + b1_ref[0]
        glu = jnp.minimum(h[:, :D_FF], SWIGLU_LIMIT)
        lin = jnp.clip(h[:, D_FF:], -SWIGLU_LIMIT, SWIGLU_LIMIT)
        act = glu * _sigmoid(SWIGLU_ALPHA * glu) * (lin + 1.0)
        y = jnp.dot(act.astype(BF16), w2_ref[0].astype(BF16),
                    preferred_element_type=F32) + b2_ref[0]
        y_ref[...] = y.astype(y_ref.dtype)

    @pl.when(i >= nu_ref[0])
    def _unused():
        y_ref[...] = jnp.zeros(y_ref.shape, y_ref.dtype)


def _moe(xs, block_expert, n_used, w1, b1, w2, b2, tm):
    n_rows, d = xs.shape
    e, _, f2 = w1.shape
    n_blocks = n_rows // tm
    grid_spec = pltpu.PrefetchScalarGridSpec(
        num_scalar_prefetch=2,
        grid=(n_blocks,),
        in_specs=[pl.BlockSpec((tm, d), lambda i, be, nu: (i, 0)),
                  pl.BlockSpec((1, d, f2), lambda i, be, nu: (be[i], 0, 0)),
                  pl.BlockSpec((1, 1, f2), lambda i, be, nu: (be[i], 0, 0)),
                  pl.BlockSpec((1, f2 // 2, d), lambda i, be, nu: (be[i], 0, 0)),
                  pl.BlockSpec((1, 1, d), lambda i, be, nu: (be[i], 0, 0))],
        out_specs=pl.BlockSpec((tm, d), lambda i, be, nu: (i, 0)),
    )
    return pl.pallas_call(
        _moe_kernel,
        grid_spec=grid_spec,
        out_shape=jax.ShapeDtypeStruct((n_rows, d), BF16),
        compiler_params=_cparams(("arbitrary",)),
        name="moe_experts",
    )(block_expert, n_used, xs, w1, b1.reshape(e, 1, f2), w2, b2.reshape(e, 1, d))


def _route(logits, tm):
    n = logits.shape[0]
    top_val, top_idx = lax.top_k(logits, TOP_K)
    gates = jax.nn.softmax(top_val, axis=-1)
    nk = n * TOP_K
    sel = top_idx[:, :, None] == jnp.arange(N_EXPERTS, dtype=top_idx.dtype)[None, None, :]
    cnt = jnp.any(sel, axis=1).astype(BF16).reshape(n // RANK_CHUNK, RANK_CHUNK, N_EXPERTS)
    tri = jnp.asarray(np.tril(np.ones((RANK_CHUNK, RANK_CHUNK), np.float32), -1), BF16)
    within = jnp.einsum('ij,cjk->cik', tri, cnt, preferred_element_type=F32)
    tot = jnp.sum(cnt.astype(F32), axis=1)
    offs = jnp.cumsum(tot, axis=0) - tot
    ahead = (within + offs[:, None, :]).reshape(n, 1, N_EXPERTS)
    counts = (offs[-1] + tot[-1]).astype(jnp.int32)
    padded = (counts + tm - 1) // tm * tm
    pend = jnp.cumsum(padded)
    pstart = (pend - padded).astype(F32)
    dest = jnp.sum(jnp.where(sel, ahead + pstart[None, None, :], 0.0), axis=2).astype(jnp.int32)
    n_blocks = nk // tm + N_EXPERTS
    n_rows = n_blocks * tm
    row_tok = jnp.zeros((n_rows,), jnp.int32).at[dest.reshape(nk)].set(
        jnp.arange(nk, dtype=jnp.int32) // TOP_K)
    block_start = jnp.arange(n_blocks, dtype=jnp.int32) * tm
    block_expert = jnp.minimum(
        jnp.sum((pend[None, :] <= block_start[:, None]).astype(jnp.int32), axis=1),
        N_EXPERTS - 1).astype(jnp.int32)
    n_used = (pend[-1] // tm).astype(jnp.int32).reshape(1)
    return dest, row_tok, gates, block_expert, n_used


def _combine_kernel(x_ref, y_ref, gt_ref, g2_ref, o_ref):
    reps = x_ref.shape[1] // LANES
    acc = None
    for k in range(TOP_K):
        gate = jnp.concatenate([gt_ref[k]] * reps, axis=1)
        term = gate * y_ref[k].astype(F32)
        acc = term if acc is None else acc + term
    o_ref[...] = x_ref[...] + g2_ref[...] * acc


def _combine(x2, t, yk, gates_b, mod4, tm=512):
    n, d = x2.shape
    tm = min(tm, t)
    tpb = t // tm
    return pl.pallas_call(
        _combine_kernel,
        grid=(n // tm,),
        in_specs=[pl.BlockSpec((tm, d), lambda i: (i, 0)),
                  pl.BlockSpec((TOP_K, tm, d), lambda i: (0, i, 0)),
                  pl.BlockSpec((TOP_K, tm, LANES), lambda i: (0, i, 0)),
                  pl.BlockSpec((None, None, 1, d), lambda i: (i // tpb, 5, 0, 0))],
        out_specs=pl.BlockSpec((tm, d), lambda i: (i, 0)),
        out_shape=jax.ShapeDtypeStruct((n, d), F32),
        compiler_params=_cparams(("parallel",)),
        name="combine",
    )(x2, yk, gates_b, mod4)


def _tile_gain(g, reps):
    return jnp.tile(g.astype(F32), reps).reshape(1, -1)


def _trunk(x, mods, p, moe_tm=MOE_TM):
    b, t, d = x.shape
    n = b * t
    depth = p["norm1"].shape[0]
    x2 = x.reshape(n, d)
    w = GROUP_W
    for l in range(depth):
        lambda_init = 0.8 - 0.6 * math.exp(-0.3 * l)
        mod4 = mods[l].reshape(b, N_MOD, 1, d)
        q_sel, kt, v_aug, ub, uc, qd, kd, vd = _inproj(
            x2, t, p["norm1"][l].reshape(1, d), mod4, p["w_in"][l].astype(BF16),
            _tile_gain(p["a_q_gain"][l], w // A_QK), _tile_gain(p["a_k_gain"][l], w // A_QK),
            _tile_gain(p["d_q_gain"][l], w // D_HDIM), _tile_gain(p["d_k_gain"][l], w // D_HDIM))

        sub_gain = jnp.concatenate([p["a_sub_gain"][l].astype(F32),
                                    jnp.zeros((A_VDIM,), F32)]).reshape(1, 2 * A_VDIM)
        oa = _attn_a(q_sel.reshape(b, t, -1), kt, v_aug.reshape(b, t, -1),
                     p["a_lambda"][l], sub_gain, lambda_init)

        ob, oc = _conv(ub.reshape(b, t, 2 * w), uc.reshape(b, t, 3 * w), p["b_dw"][l],
                       p["b_dw_bias"][l].reshape(1, w), p["b_ln_g"][l].reshape(1, w),
                       p["b_ln_b"][l].reshape(1, w), p["c_conv"][l])

        od = _attn_d(qd.reshape(b, t, w), kd.reshape(b, t, w), vd.reshape(b, t, w),
                     _d_bias(p["d_rpb"][l]))

        x2, h2, logits = _outproj(
            x2, t, oa.reshape(n, w), ob.reshape(n, w), oc.reshape(n, w), od.reshape(n, w),
            p["w_out"][l].astype(BF16), mod4, p["norm2"][l].reshape(1, d),
            p["w_router"][l], p["b_router"][l].reshape(1, N_EXPERTS))

        dest, row_tok, gates, block_expert, n_used = _route(logits, moe_tm)
        xs = h2[row_tok]
        y = _moe(xs, block_expert, n_used, p["w1"][l], p["b1"][l], p["w2"][l], p["b2"][l],
                 moe_tm)
        yk = y[dest.T]
        gates_b = jnp.broadcast_to(gates.T[:, :, None], (TOP_K, n, LANES))
        x2 = _combine(x2, t, yk, gates_b, mod4)
    return x2.reshape(b, t, d)


def kernel(x_prompt, x_sample, c_prompt, c_sample, norm1, norm2, w_ada, b_ada, w_in, w_out, a_q_gain, a_k_gain, a_lambda, a_sub_gain, b_dw, b_dw_bias, b_ln_g, b_ln_b, c_conv, d_q_gain, d_k_gain, d_rpb, w_router, b_router, w1, b1, w2, b2):
    p = dict(norm1=norm1, norm2=norm2, w_in=w_in, w_out=w_out, a_q_gain=a_q_gain,
             a_k_gain=a_k_gain, a_lambda=a_lambda, a_sub_gain=a_sub_gain, b_dw=b_dw,
             b_dw_bias=b_dw_bias, b_ln_g=b_ln_g, b_ln_b=b_ln_b, c_conv=c_conv,
             d_q_gain=d_q_gain, d_k_gain=d_k_gain, d_rpb=d_rpb, w_router=w_router,
             b_router=b_router, w1=w1, b1=b1, w2=w2, b2=b2)
    bp = c_prompt.shape[0]
    bs = c_sample.shape[0]
    rows = -(-(bp + bs) // 8) * 8
    c_all = jnp.concatenate([c_prompt, c_sample,
                             jnp.zeros((rows - bp - bs, c_prompt.shape[1]), F32)], axis=0)
    mods = _ada(c_all, w_ada, b_ada)
    y_prompt = _trunk(x_prompt, mods[:, :bp], p)
    y_sample = _trunk(x_sample, mods[:, bp:bp + bs], p)
    return (y_prompt, y_sample)
```

```python
import functools
import math

import numpy as np
import jax
import jax.numpy as jnp
from jax import lax
from jax.experimental import pallas as pl
from jax.experimental.pallas import tpu as pltpu

F32 = jnp.float32
BF16 = jnp.bfloat16

D_MODEL = 1024
GROUP_W = 256
A_HEADS = 4
A_VDIM = 64
A_QK = 32
B_KERNEL = 31
C_KERNEL = 3
D_HEADS = 4
D_HDIM = 64
GRID_W = 64
WIN_ROWS = 8
WIN_COLS = 16
IN_COLS = 11 * GROUP_W
N_EXPERTS = 32
TOP_K = 4
D_FF = 1024
SWIGLU_ALPHA = 1.702
SWIGLU_LIMIT = 7.0
N_MOD = 6
NORM_EPS = 1e-6
LN_EPS = 1e-5

LOG2E = 1.4426950408889634
NEG_BIG = -1e30
LANES = 128
A_AUG = 16
A_KDIM = A_QK + A_AUG
A_POS_SPLIT = 256
HALO = 16
D_QROWS = 4
D_KROWS = 12
D_TQ = D_QROWS * GRID_W
D_TK = D_KROWS * GRID_W
MOE_TM = 512
RANK_CHUNK = 128
VMEM_LIMIT = 56 * 1024 * 1024


def _cparams(sems, vmem=VMEM_LIMIT):
    return pltpu.CompilerParams(dimension_semantics=sems, vmem_limit_bytes=vmem)


def _sigmoid(x):
    return 1.0 / (1.0 + jnp.exp(-x))


def _ada_kernel(c_ref, w_ref, b_ref, o_ref):
    c = c_ref[...]
    s = c * _sigmoid(c)
    o_ref[0] = jnp.dot(s.astype(BF16), w_ref[0].astype(BF16),
                       preferred_element_type=F32) + b_ref[0]


def _ada(c_all, w_ada, b_ada):
    depth, d, n6 = w_ada.shape
    rows = c_all.shape[0]
    tn = 1536
    return pl.pallas_call(
        _ada_kernel,
        grid=(depth, n6 // tn),
        in_specs=[pl.BlockSpec((rows, d), lambda l, j: (0, 0)),
                  pl.BlockSpec((1, d, tn), lambda l, j: (l, 0, j)),
                  pl.BlockSpec((1, 1, tn), lambda l, j: (l, 0, j))],
        out_specs=pl.BlockSpec((1, rows, tn), lambda l, j: (l, 0, j)),
        out_shape=jax.ShapeDtypeStruct((depth, rows, n6), F32),
        compiler_params=_cparams(("parallel", "parallel")),
        name="ada_mod",
    )(c_all, w_ada, b_ada.reshape(depth, 1, n6))


def _inproj_kernel(x_ref, g_ref, sh_ref, sc_ref, w_ref, g32_ref, g64_ref,
                   aq_ref, ak_ref, dq_ref, dk_ref, selq_ref, selv_ref, vone_ref,
                   qa_ref, kt_ref, va_ref, ub_ref, uc_ref, qd_ref, kd_ref, vd_ref):
    x = x_ref[...]
    ms = jnp.mean(x * x, axis=-1, keepdims=True)
    h = x * lax.rsqrt(ms + NORM_EPS) * g_ref[...]
    h = h * (1.0 + sc_ref[...]) + sh_ref[...]
    hb = h.astype(BF16)

    def proj(lo, hi):
        return jnp.dot(hb, w_ref[:, lo:hi], preferred_element_type=F32)

    def group_norm(u, gmat_ref, gain_ref, post):
        u2 = u * u
        u2_hi = u2.astype(BF16)
        u2_lo = (u2 - u2_hi.astype(F32)).astype(BF16)
        gmat = gmat_ref[...]
        gms = (jnp.dot(u2_hi, gmat, preferred_element_type=F32)
               + jnp.dot(u2_lo, gmat, preferred_element_type=F32))
        return u * lax.rsqrt(gms + NORM_EPS) * (gain_ref[...] * post)

    w = GROUP_W
    a_scale = (A_QK ** -0.5) * LOG2E
    d_scale = (D_HDIM ** -0.5) * LOG2E
    qa = group_norm(proj(0, w), g32_ref, aq_ref, a_scale).astype(BF16)
    qa_ref[...] = jnp.dot(qa, selq_ref[...], preferred_element_type=F32).astype(BF16)
    kt_ref[...] = group_norm(proj(w, 2 * w), g32_ref, ak_ref, 1.0).T.astype(BF16)
    va = proj(2 * w, 3 * w).astype(BF16)
    va_ref[...] = (jnp.dot(va, selv_ref[...], preferred_element_type=F32)
                   + vone_ref[...]).astype(BF16)
    ub_ref[...] = proj(3 * w, 5 * w)
    uc_ref[...] = proj(5 * w, 8 * w)
    qd_ref[...] = group_norm(proj(8 * w, 9 * w), g64_ref, dq_ref, d_scale).astype(BF16)
    kd_ref[...] = group_norm(proj(9 * w, 10 * w), g64_ref, dk_ref, 1.0).astype(BF16)
    vd_ref[...] = proj(10 * w, 11 * w).astype(BF16)


def _group_mean_matrix(group):
    idx = np.arange(GROUP_W) // group
    return jnp.asarray((idx[:, None] == idx[None, :]).astype(np.float32) / group, dtype=BF16)


def _inproj(x2, t, norm_g, mod4, w_in, aq, ak, dq, dk, tm=512):
    n, d = x2.shape
    tm = min(tm, t)
    tpb = t // tm
    row = lambda i: (i, 0)
    const = lambda i: (0, 0)
    w = GROUP_W
    mod_spec = lambda k: pl.BlockSpec((None, None, 1, d), lambda i: (i // tpb, k, 0, 0))
    n_maps = 2 * A_HEADS
    chan = np.arange(w)
    selq = np.zeros((w, n_maps * LANES), np.float32)
    selq[chan, LANES * (chan // A_QK) + chan % A_QK] = 1.0
    selv = np.zeros((w, A_HEADS * LANES), np.float32)
    selv[chan, LANES * (chan // A_VDIM) + chan % A_VDIM] = 1.0
    vone = np.zeros((1, A_HEADS * LANES), np.float32)
    vone[0, LANES * np.arange(A_HEADS) + A_VDIM] = 1.0
    out_cols = [(n_maps * LANES, BF16), None, (A_HEADS * LANES, BF16), (2 * w, F32),
                (3 * w, F32), (w, BF16), (w, BF16), (w, BF16)]
    out_specs = [pl.BlockSpec((w, tm), lambda i: (0, i)) if c is None
                 else pl.BlockSpec((tm, c[0]), row) for c in out_cols]
    out_shape = [jax.ShapeDtypeStruct((w, n), BF16) if c is None
                 else jax.ShapeDtypeStruct((n, c[0]), c[1]) for c in out_cols]
    return pl.pallas_call(
        _inproj_kernel,
        grid=(n // tm,),
        in_specs=[pl.BlockSpec((tm, d), row),
                  pl.BlockSpec((1, d), const),
                  mod_spec(0), mod_spec(1),
                  pl.BlockSpec((d, IN_COLS), const),
                  pl.BlockSpec((w, w), const), pl.BlockSpec((w, w), const),
                  pl.BlockSpec((1, w), const), pl.BlockSpec((1, w), const),
                  pl.BlockSpec((1, w), const), pl.BlockSpec((1, w), const),
                  pl.BlockSpec((w, n_maps * LANES), const),
                  pl.BlockSpec((w, A_HEADS * LANES), const),
                  pl.BlockSpec((1, A_HEADS * LANES), const)],
        out_specs=out_specs,
        out_shape=out_shape,
        compiler_params=_cparams(("parallel",)),
        name="inproj",
    )(x2, norm_g, mod4, mod4, w_in, _group_mean_matrix(A_QK), _group_mean_matrix(D_HDIM),
      aq, ak, dq, dk, jnp.asarray(selq, BF16), jnp.asarray(selv, BF16), jnp.asarray(vone, F32))


def _bf16_pieces(x, n=3):
    out = []
    for _ in range(n):
        piece = float(np.asarray(x, dtype=BF16).astype(np.float32))
        out.append(piece)
        x = x - piece
    return out


def _alibi_slope(h):
    slope = 2.0 ** (-8.0 * (h + 1) / A_HEADS)
    assert math.log2(slope) == round(math.log2(slope)), "slopes must be powers of two"
    return slope


def _a_constants(tk):
    pos = np.arange(tk)
    lo = (pos % A_POS_SPLIT).astype(np.float32)
    hi = (pos - pos % A_POS_SPLIT).astype(np.float32)
    assert hi.max() / A_POS_SPLIT <= A_POS_SPLIT
    pieces = _bf16_pieces(LOG2E)
    k_rows = np.zeros((A_AUG, tk), np.float32)
    k_rows[0:3] = lo
    k_rows[3:6] = hi
    for r, piece in enumerate(pieces):
        k_rows[6 + r] = piece
        k_rows[9 + r] = piece
    q_cols = np.zeros((A_HEADS, 2, tk, A_KDIM), np.float32)
    for h in range(A_HEADS):
        slope = _alibi_slope(h)
        for var, sign in enumerate((1.0, -1.0)):
            for r, piece in enumerate(pieces):
                q_cols[h, var, :, A_QK + r] = -sign * slope * piece
                q_cols[h, var, :, A_QK + 3 + r] = -sign * slope * piece
                q_cols[h, var, :, A_QK + 6 + r] = sign * slope * lo
                q_cols[h, var, :, A_QK + 9 + r] = sign * slope * hi
    return jnp.asarray(q_cols, BF16), jnp.asarray(k_rows, BF16)


def _attn_a_kernel(q_ref, kt_ref, v_ref, qc_ref, kr_ref, lam_ref, sg_ref, o_ref,
                   qa_sc, kt_sc, m_sc, acc_sc, *, tq, tk):
    i = pl.program_id(1)
    j = pl.program_id(2)
    nj = pl.num_programs(2)
    n_maps = 2 * A_HEADS
    jd = (i * tq) // tk

    @pl.when(j == 0)
    def _init():
        for hc in range(n_maps):
            base = q_ref[0, :, LANES * hc:LANES * hc + A_KDIM].astype(F32)
            for var in range(2):
                qa_sc[hc, var] = (base + qc_ref[hc // 2, var].astype(F32)).astype(BF16)
            kt_sc[hc, A_QK:A_KDIM, :] = kr_ref[...]
        m_sc[...] = jnp.full(m_sc.shape, NEG_BIG, F32)
        acc_sc[...] = jnp.zeros(acc_sc.shape, F32)

    for hc in range(n_maps):
        kt_sc[hc, 0:A_QK, :] = kt_ref[A_QK * hc:A_QK * (hc + 1), :]

    def softmax_step(hc, s, rho):
        m_prev = m_sc[hc]
        m_new = jnp.maximum(m_prev, jnp.max(s, axis=-1, keepdims=True) + rho)
        alpha = jnp.exp2(m_prev - m_new)
        mu = m_new - rho
        p = jnp.exp2(s - jnp.concatenate([mu] * (tk // LANES), axis=1))
        h = hc // 2
        acc_sc[hc] = alpha * acc_sc[hc] + jnp.dot(p.astype(BF16),
                                                  v_ref[0, :, LANES * h:LANES * (h + 1)],
                                                  preferred_element_type=F32)
        m_sc[hc] = m_new

    @pl.when(j != jd)
    def _off_diagonal():
        var = jnp.where(j > jd, 0, 1)
        tile_dist = (jnp.abs(j - jd) * tk).astype(F32)
        for hc in range(n_maps):
            s = jnp.dot(qa_sc[hc, var], kt_sc[hc], preferred_element_type=F32)
            softmax_step(hc, s, tile_dist * (-_alibi_slope(hc // 2) * LOG2E))

    @pl.when(j == jd)
    def _diagonal():
        for hc in range(n_maps):
            kt = kt_sc[hc]
            s = jnp.minimum(jnp.dot(qa_sc[hc, 0], kt, preferred_element_type=F32),
                            jnp.dot(qa_sc[hc, 1], kt, preferred_element_type=F32))
            softmax_step(hc, s, 0.0)

    @pl.when(j == nj - 1)
    def _finish():
        lv = lam_ref[...]
        lambda_init = lv[4:5, 0:1]
        lam = (jnp.exp(jnp.sum(lv[0:1] * lv[1:2], axis=-1, keepdims=True))
               - jnp.exp(jnp.sum(lv[2:3] * lv[3:4], axis=-1, keepdims=True)) + lambda_init)
        lane = lax.broadcasted_iota(jnp.int32, (tq, 2 * A_VDIM), 1)
        heads = []
        for h in range(A_HEADS):
            a1 = acc_sc[2 * h]
            a2 = acc_sc[2 * h + 1]
            l1 = jnp.sum(jnp.where(lane == A_VDIM, a1, 0.0), axis=-1, keepdims=True)
            l2 = jnp.sum(jnp.where(lane == A_VDIM, a2, 0.0), axis=-1, keepdims=True)
            o = jnp.where(lane < A_VDIM, a1 / l1 - lam * (a2 / l2), 0.0)
            ms = jnp.sum(o * o, axis=-1, keepdims=True) * (1.0 / A_VDIM)
            heads.append(o * lax.rsqrt(ms + NORM_EPS) * (sg_ref[...] * (1.0 - lambda_init)))
        o_ref[0, :, 0:128] = heads[0] + pltpu.roll(heads[1], A_VDIM, 1)
        o_ref[0, :, 128:256] = heads[2] + pltpu.roll(heads[3], A_VDIM, 1)


def _attn_a(q_sel, kt, v_aug, lam, sub_gain, lambda_init, tq=512, tk=1024):
    b, t, _ = q_sel.shape
    lam = jnp.concatenate([lam.astype(F32), jnp.full((1, A_QK), lambda_init, F32)], axis=0)
    w = GROUP_W
    tq = min(tq, t)
    tk = min(tk, t)
    assert tk % tq == 0 and tk % A_POS_SPLIT == 0 and t % tk == 0
    n_maps = 2 * A_HEADS
    nkt = t // tk
    qpt = tk // tq
    q_cols, k_rows = _a_constants(tk)
    return pl.pallas_call(
        functools.partial(_attn_a_kernel, tq=tq, tk=tk),
        grid=(b, t // tq, nkt),
        in_specs=[pl.BlockSpec((1, tq, n_maps * LANES), lambda bi, i, j: (bi, i, 0)),
                  pl.BlockSpec((w, tk), lambda bi, i, j: (0, bi * nkt + j)),
                  pl.BlockSpec((1, tk, A_HEADS * LANES), lambda bi, i, j: (bi, j, 0)),
                  pl.BlockSpec((A_HEADS, 2, tq, A_KDIM), lambda bi, i, j: (0, 0, i % qpt, 0)),
                  pl.BlockSpec((A_AUG, tk), lambda bi, i, j: (0, 0)),
                  pl.BlockSpec((5, A_QK), lambda bi, i, j: (0, 0)),
                  pl.BlockSpec((1, 2 * A_VDIM), lambda bi, i, j: (0, 0))],
        out_specs=pl.BlockSpec((1, tq, w), lambda bi, i, j: (bi, i, 0)),
        out_shape=jax.ShapeDtypeStruct((b, t, w), F32),
        scratch_shapes=[pltpu.VMEM((n_maps, 2, tq, A_KDIM), BF16),
                        pltpu.VMEM((n_maps, A_KDIM, tk), BF16),
                        pltpu.VMEM((n_maps, tq, LANES), F32),
                        pltpu.VMEM((n_maps, tq, 2 * A_VDIM), F32)],
        compiler_params=_cparams(("parallel", "parallel", "arbitrary")),
        name="attn_a",
    )(q_sel, kt, v_aug, q_cols, k_rows, lam, sub_gain)


def _conv_kernel(ub_ref, ubp_ref, ubn_ref, uc_ref, ucp_ref, ucn_ref,
                 wdw_ref, bdw_ref, lng_ref, lnb_ref, wc_ref,
                 ob_ref, oc_ref, gext, pext, *, tt, rc):
    i = pl.program_id(1)
    n = pl.num_programs(1)
    has_prev = jnp.where(i > 0, 1.0, 0.0)
    has_next = jnp.where(i < n - 1, 1.0, 0.0)
    w = GROUP_W

    def glu(u):
        return u[:, :w] * _sigmoid(u[:, w:])

    def gated(u):
        return u[:, w:2 * w] * u[:, 2 * w:]

    gext[0:HALO, :] = glu(ubp_ref[0]) * has_prev
    gext[HALO:HALO + tt, :] = glu(ub_ref[0])
    gext[HALO + tt:2 * HALO + tt, :] = glu(ubn_ref[0]) * has_next
    pext[0:HALO, :] = gated(ucp_ref[0]) * has_prev
    pext[HALO:HALO + tt, :] = gated(uc_ref[0])
    pext[HALO + tt:2 * HALO + tt, :] = gated(ucn_ref[0]) * has_next

    pad_b = (B_KERNEL - 1) // 2
    pad_c = (C_KERNEL - 1) // 2
    for r0 in range(0, tt, rc):
        acc = jnp.zeros((rc, w), F32) + bdw_ref[...]
        for k in range(B_KERNEL):
            off = HALO + r0 + k - pad_b
            acc = acc + wdw_ref[k:k + 1, :] * gext[off:off + rc, :]
        mu = jnp.mean(acc, axis=-1, keepdims=True)
        cen = acc - mu
        var = jnp.mean(cen * cen, axis=-1, keepdims=True)
        y = cen * lax.rsqrt(var + LN_EPS) * lng_ref[...] + lnb_ref[...]
        ob_ref[0, r0:r0 + rc, :] = y * _sigmoid(y)

        accc = jnp.zeros((rc, w), F32)
        for k in range(C_KERNEL):
            off = HALO + r0 + k - pad_c
            accc = accc + wc_ref[k:k + 1, :] * pext[off:off + rc, :]
        oc_ref[0, r0:r0 + rc, :] = uc_ref[0, r0:r0 + rc, 0:w] * accc


def _conv(ub, uc, w_dw, b_dw, ln_g, ln_b, w_c, tt=256, rc=64):
    b, t, _ = ub.shape
    tt = min(tt, t)
    w = GROUP_W
    hb = tt // HALO
    last = t // HALO - 1
    cur = lambda bi, i: (bi, i, 0)
    prev = lambda bi, i: (bi, jnp.maximum(i * hb - 1, 0), 0)
    nxt = lambda bi, i: (bi, jnp.minimum((i + 1) * hb, last), 0)
    const = lambda bi, i: (0, 0)
    return pl.pallas_call(
        functools.partial(_conv_kernel, tt=tt, rc=rc),
        grid=(b, t // tt),
        in_specs=[pl.BlockSpec((1, tt, 2 * w), cur),
                  pl.BlockSpec((1, HALO, 2 * w), prev),
                  pl.BlockSpec((1, HALO, 2 * w), nxt),
                  pl.BlockSpec((1, tt, 3 * w), cur),
                  pl.BlockSpec((1, HALO, 3 * w), prev),
                  pl.BlockSpec((1, HALO, 3 * w), nxt),
                  pl.BlockSpec((B_KERNEL, w), const),
                  pl.BlockSpec((1, w), const), pl.BlockSpec((1, w), const),
                  pl.BlockSpec((1, w), const),
                  pl.BlockSpec((C_KERNEL, w), const)],
        out_specs=[pl.BlockSpec((1, tt, w), cur), pl.BlockSpec((1, tt, w), cur)],
        out_shape=[jax.ShapeDtypeStruct((b, t, w), F32), jax.ShapeDtypeStruct((b, t, w), F32)],
        scratch_shapes=[pltpu.VMEM((tt + 2 * HALO, w), F32), pltpu.VMEM((tt + 2 * HALO, w), F32)],
        compiler_params=_cparams(("parallel", "parallel")),
        name="conv_bc",
    )(ub, ub, ub, uc, uc, uc, w_dw, b_dw, ln_g, ln_b, w_c)


def _attn_d_kernel(q_ref, k_ref, v_ref, b_ref, o_ref, *, t):
    g = pl.program_id(1)
    start = pl.multiple_of(jnp.clip((g - 1) * D_TQ, 0, t - D_TK), D_TQ)
    q = q_ref[0].astype(F32)
    k = k_ref[0, pl.ds(start, D_TK), :]
    v = v_ref[0, pl.ds(start, D_TK), :]
    lane_h = lax.broadcasted_iota(jnp.int32, (D_TQ, GROUP_W), 1) // D_HDIM
    out = jnp.zeros((D_TQ, GROUP_W), F32)
    for h in range(D_HEADS):
        qm = jnp.where(lane_h == h, q, 0.0).astype(BF16)
        s = lax.dot_general(qm, k, (((1,), (1,)), ((), ())),
                            preferred_element_type=F32) + b_ref[0, h]
        m = jnp.max(s, axis=-1, keepdims=True)
        p = jnp.exp2(s - m)
        l = jnp.sum(p, axis=-1, keepdims=True)
        o = jnp.dot(p.astype(BF16), v, preferred_element_type=F32)
        out = jnp.where(lane_h == h, o / l, out)
    o_ref[0] = out


def _d_tables():
    qc = np.arange(GRID_W)
    kc = np.arange(GRID_W)
    cs = np.clip(qc - WIN_COLS // 2, 0, GRID_W - WIN_COLS)
    ok_c = (kc[None, :] >= cs[:, None]) & (kc[None, :] < cs[:, None] + WIN_COLS)
    crel = kc[None, :] - qc[:, None] + WIN_COLS - 1
    n_c = 2 * WIN_COLS - 1
    col_sel = (crel[None] == np.arange(n_c)[:, None, None]).astype(np.float32)
    rows = 3 * D_KROWS
    groups = rows // D_QROWS
    n_r = 2 * WIN_ROWS - 1
    row_sel = np.zeros((3, D_QROWS, D_KROWS, n_r), np.float32)
    ok_r = np.zeros((3, D_QROWS, D_KROWS), bool)
    for var, g in enumerate((0, groups // 2, groups - 1)):
        start_row = int(np.clip(D_QROWS * (g - 1), 0, rows - D_KROWS))
        for a in range(D_QROWS):
            qr = D_QROWS * g + a
            rs = int(np.clip(qr - WIN_ROWS // 2, 0, rows - WIN_ROWS))
            for kb in range(D_KROWS):
                kr = start_row + kb
                if rs <= kr < rs + WIN_ROWS:
                    ok_r[var, a, kb] = True
                    row_sel[var, a, kb, kr - qr + WIN_ROWS - 1] = 1.0
    valid = ok_r[:, :, None, :, None] & ok_c[None, None, :, None, :]
    return col_sel, row_sel, valid.reshape(3, D_TQ, D_TK)


def _d_bias(rpb):
    col_sel, row_sel, valid = _d_tables()
    hp = lax.Precision.HIGHEST
    tab = jnp.einsum('hrc,cqk->hrqk', rpb.astype(F32), col_sel, precision=hp)
    bias = jnp.einsum('vabr,hrqk->vhaqbk', row_sel, tab, precision=hp)
    bias = bias.reshape(3, D_HEADS, D_TQ, D_TK) * LOG2E
    return jnp.where(valid[:, None], bias, NEG_BIG)


def _attn_d(qd, kd, vd, bias):
    b, t, w = qd.shape
    assert t % D_TQ == 0 and t >= D_TK
    groups = t // D_TQ
    variant = lambda bi, g: (jnp.where(g == 0, 0, jnp.where(g == groups - 1, 2, 1)), 0, 0, 0)
    return pl.pallas_call(
        functools.partial(_attn_d_kernel, t=t),
        grid=(b, groups),
        in_specs=[pl.BlockSpec((1, D_TQ, w), lambda bi, g: (bi, g, 0)),
                  pl.BlockSpec((1, t, w), lambda bi, g: (bi, 0, 0)),
                  pl.BlockSpec((1, t, w), lambda bi, g: (bi, 0, 0)),
                  pl.BlockSpec((1, D_HEADS, D_TQ, D_TK), variant)],
        out_specs=pl.BlockSpec((1, D_TQ, w), lambda bi, g: (bi, g, 0)),
        out_shape=jax.ShapeDtypeStruct((b, t, w), F32),
        compiler_params=_cparams(("parallel", "arbitrary")),
        name="attn_d",
    )(qd, kd, vd, bias)


def _outproj_kernel(x_ref, oa_ref, ob_ref, oc_ref, od_ref, w_ref, g1_ref, sh_ref, sc_ref,
                    n2_ref, wr_ref, br_ref, xo_ref, h2_ref, lg_ref):
    w = GROUP_W
    mix = jnp.dot(oa_ref[...].astype(BF16), w_ref[0:w, :], preferred_element_type=F32)
    mix += jnp.dot(ob_ref[...].astype(BF16), w_ref[w:2 * w, :], preferred_element_type=F32)
    mix += jnp.dot(oc_ref[...].astype(BF16), w_ref[2 * w:3 * w, :], preferred_element_type=F32)
    mix += jnp.dot(od_ref[...].astype(BF16), w_ref[3 * w:4 * w, :], preferred_element_type=F32)
    xn = x_ref[...] + g1_ref[...] * mix
    xo_ref[...] = xn
    ms = jnp.mean(xn * xn, axis=-1, keepdims=True)
    h2 = xn * lax.rsqrt(ms + NORM_EPS) * n2_ref[...]
    h2 = h2 * (1.0 + sc_ref[...]) + sh_ref[...]
    h_hi = h2.astype(BF16)
    h_lo = (h2 - h_hi.astype(F32)).astype(BF16)
    wr = wr_ref[...]
    w_hi = wr.astype(BF16)
    w_lo = (wr - w_hi.astype(F32)).astype(BF16)
    lg = (jnp.dot(h_hi, w_hi, preferred_element_type=F32)
          + jnp.dot(h_hi, w_lo, preferred_element_type=F32)
          + jnp.dot(h_lo, w_hi, preferred_element_type=F32))
    lg_ref[...] = lg + br_ref[...]
    h2_ref[...] = h_hi


def _outproj(x2, t, oa, ob, oc, od, w_out, mod4, norm_g, w_router, b_router, tm=512):
    n, d = x2.shape
    tm = min(tm, t)
    tpb = t // tm
    w = GROUP_W
    row = lambda i: (i, 0)
    const = lambda i: (0, 0)
    mod_spec = lambda k: pl.BlockSpec((None, None, 1, d), lambda i: (i // tpb, k, 0, 0))
    return pl.pallas_call(
        _outproj_kernel,
        grid=(n // tm,),
        in_specs=[pl.BlockSpec((tm, d), row),
                  pl.BlockSpec((tm, w), row), pl.BlockSpec((tm, w), row),
                  pl.BlockSpec((tm, w), row), pl.BlockSpec((tm, w), row),
                  pl.BlockSpec((d, d), const),
                  mod_spec(2), mod_spec(3), mod_spec(4),
                  pl.BlockSpec((1, d), const),
                  pl.BlockSpec((d, N_EXPERTS), const),
                  pl.BlockSpec((1, N_EXPERTS), const)],
        out_specs=[pl.BlockSpec((tm, d), row), pl.BlockSpec((tm, d), row),
                   pl.BlockSpec((tm, N_EXPERTS), row)],
        out_shape=[jax.ShapeDtypeStruct((n, d), F32), jax.ShapeDtypeStruct((n, d), BF16),
                   jax.ShapeDtypeStruct((n, N_EXPERTS), F32)],
        compiler_params=_cparams(("parallel",)),
        name="outproj",
    )(x2, oa, ob, oc, od, w_out, mod4, mod4, mod4, norm_g, w_router, b_router)


def _moe_kernel(be_ref, nu_ref, x_ref, w1_ref, b1_ref, w2_ref, b2_ref, y_ref):
    i = pl.program_id(0)

    @pl.when(i < nu_ref[0])
    def _compute():
        h = jnp.dot(x_ref[...], w1_ref[0].astype(BF16), preferred_element_type=F32) + b1_ref[0]
        glu = jnp.minimum(h[:, :D_FF], SWIGLU_LIMIT)
        lin = jnp.clip(h[:, D_FF:], -SWIGLU_LIMIT, SWIGLU_LIMIT)
        act = glu * _sigmoid(SWIGLU_ALPHA * glu) * (lin + 1.0)
        y = jnp.dot(act.astype(BF16), w2_ref[0].astype(BF16),
                    preferred_element_type=F32) + b2_ref[0]
        y_ref[...] = y.astype(y_ref.dtype)

    @pl.when(i >= nu_ref[0])
    def _unused():
        y_ref[...] = jnp.zeros(y_ref.shape, y_ref.dtype)


def _moe(xs, block_expert, n_used, w1, b1, w2, b2, layer, tm):
    n_rows, d = xs.shape
    depth, e, _, f2 = w1.shape
    n_blocks = n_rows // tm
    grid_spec = pltpu.PrefetchScalarGridSpec(
        num_scalar_prefetch=2,
        grid=(n_blocks,),
        in_specs=[pl.BlockSpec((tm, d), lambda i, be, nu: (i, 0)),
                  pl.BlockSpec((None, 1, d, f2), lambda i, be, nu: (layer, be[i], 0, 0)),
                  pl.BlockSpec((None, 1, 1, f2), lambda i, be, nu: (layer, be[i], 0, 0)),
                  pl.BlockSpec((None, 1, f2 // 2, d), lambda i, be, nu: (layer, be[i], 0, 0)),
                  pl.BlockSpec((None, 1, 1, d), lambda i, be, nu: (layer, be[i], 0, 0))],
        out_specs=pl.BlockSpec((tm, d), lambda i, be, nu: (i, 0)),
    )
    return pl.pallas_call(
        _moe_kernel,
        grid_spec=grid_spec,
        out_shape=jax.ShapeDtypeStruct((n_rows, d), BF16),
        compiler_params=_cparams(("arbitrary",)),
        name="moe_experts",
    )(block_expert, n_used, xs, w1, b1.reshape(depth, e, 1, f2), w2, b2.reshape(depth, e, 1, d))


def _route(logits, tm):
    n = logits.shape[0]
    top_val, top_idx = lax.top_k(logits, TOP_K)
    gates = jax.nn.softmax(top_val, axis=-1)
    nk = n * TOP_K
    sel = top_idx[:, :, None] == jnp.arange(N_EXPERTS, dtype=top_idx.dtype)[None, None, :]
    cnt = jnp.any(sel, axis=1).astype(BF16).reshape(n // RANK_CHUNK, RANK_CHUNK, N_EXPERTS)
    tri = jnp.asarray(np.tril(np.ones((RANK_CHUNK, RANK_CHUNK), np.float32), -1), BF16)
    within = jnp.einsum('ij,cjk->cik', tri, cnt, preferred_element_type=F32)
    tot = jnp.sum(cnt.astype(F32), axis=1)
    offs = jnp.cumsum(tot, axis=0) - tot
    ahead = (within + offs[:, None, :]).reshape(n, 1, N_EXPERTS)
    counts = (offs[-1] + tot[-1]).astype(jnp.int32)
    padded = (counts + tm - 1) // tm * tm
    pend = jnp.cumsum(padded)
    pstart = (pend - padded).astype(F32)
    dest = jnp.sum(jnp.where(sel, ahead + pstart[None, None, :], 0.0), axis=2).astype(jnp.int32)
    n_blocks = nk // tm + N_EXPERTS
    n_rows = n_blocks * tm
    row_tok = jnp.zeros((n_rows,), jnp.int32).at[dest.reshape(nk)].set(
        jnp.arange(nk, dtype=jnp.int32) // TOP_K)
    block_start = jnp.arange(n_blocks, dtype=jnp.int32) * tm
    block_expert = jnp.minimum(
        jnp.sum((pend[None, :] <= block_start[:, None]).astype(jnp.int32), axis=1),
        N_EXPERTS - 1).astype(jnp.int32)
    n_used = (pend[-1] // tm).astype(jnp.int32).reshape(1)
    return dest, row_tok, gates, block_expert, n_used


def _combine_kernel(x_ref, y_ref, gt_ref, g2_ref, o_ref):
    reps = x_ref.shape[1] // LANES
    acc = None
    for k in range(TOP_K):
        gate = jnp.concatenate([gt_ref[k]] * reps, axis=1)
        term = gate * y_ref[k].astype(F32)
        acc = term if acc is None else acc + term
    o_ref[...] = x_ref[...] + g2_ref[...] * acc


def _combine(x2, t, yk, gates_b, mod4, tm=512):
    n, d = x2.shape
    tm = min(tm, t)
    tpb = t // tm
    return pl.pallas_call(
        _combine_kernel,
        grid=(n // tm,),
        in_specs=[pl.BlockSpec((tm, d), lambda i: (i, 0)),
                  pl.BlockSpec((TOP_K, tm, d), lambda i: (0, i, 0)),
                  pl.BlockSpec((TOP_K, tm, LANES), lambda i: (0, i, 0)),
                  pl.BlockSpec((None, None, 1, d), lambda i: (i // tpb, 5, 0, 0))],
        out_specs=pl.BlockSpec((tm, d), lambda i: (i, 0)),
        out_shape=jax.ShapeDtypeStruct((n, d), F32),
        compiler_params=_cparams(("parallel",)),
        name="combine",
    )(x2, yk, gates_b, mod4)


def _tile_gain(g, reps):
    return jnp.tile(g.astype(F32), reps).reshape(1, -1)


def _trunk(x, mods, p, moe_tm=MOE_TM):
    b, t, d = x.shape
    n = b * t
    depth = p["norm1"].shape[0]
    x2 = x.reshape(n, d)
    w = GROUP_W
    for l in range(depth):
        lambda_init = 0.8 - 0.6 * math.exp(-0.3 * l)
        mod4 = mods[l].reshape(b, N_MOD, 1, d)
        q_sel, kt, v_aug, ub, uc, qd, kd, vd = _inproj(
            x2, t, p["norm1"][l].reshape(1, d), mod4, p["w_in"][l].astype(BF16),
            _tile_gain(p["a_q_gain"][l], w // A_QK), _tile_gain(p["a_k_gain"][l], w // A_QK),
            _tile_gain(p["d_q_gain"][l], w // D_HDIM), _tile_gain(p["d_k_gain"][l], w // D_HDIM))

        sub_gain = jnp.concatenate([p["a_sub_gain"][l].astype(F32),
                                    jnp.zeros((A_VDIM,), F32)]).reshape(1, 2 * A_VDIM)
        oa = _attn_a(q_sel.reshape(b, t, -1), kt, v_aug.reshape(b, t, -1),
                     p["a_lambda"][l], sub_gain, lambda_init)

        ob, oc = _conv(ub.reshape(b, t, 2 * w), uc.reshape(b, t, 3 * w), p["b_dw"][l],
                       p["b_dw_bias"][l].reshape(1, w), p["b_ln_g"][l].reshape(1, w),
                       p["b_ln_b"][l].reshape(1, w), p["c_conv"][l])

        od = _attn_d(qd.reshape(b, t, w), kd.reshape(b, t, w), vd.reshape(b, t, w),
                     _d_bias(p["d_rpb"][l]))

        x2, h2, logits = _outproj(
            x2, t, oa.reshape(n, w), ob.reshape(n, w), oc.reshape(n, w), od.reshape(n, w),
            p["w_out"][l].astype(BF16), mod4, p["norm2"][l].reshape(1, d),
            p["w_router"][l], p["b_router"][l].reshape(1, N_EXPERTS))

        dest, row_tok, gates, block_expert, n_used = _route(logits, moe_tm)
        xs = h2[row_tok]
        y = _moe(xs, block_expert, n_used, p["w1"], p["b1"], p["w2"], p["b2"], l, moe_tm)
        yk = y[dest.T]
        gates_b = jnp.broadcast_to(gates.T[:, :, None], (TOP_K, n, LANES))
        x2 = _combine(x2, t, yk, gates_b, mod4)
    return x2.reshape(b, t, d)


def kernel(x_prompt, x_sample, c_prompt, c_sample, norm1, norm2, w_ada, b_ada, w_in, w_out, a_q_gain, a_k_gain, a_lambda, a_sub_gain, b_dw, b_dw_bias, b_ln_g, b_ln_b, c_conv, d_q_gain, d_k_gain, d_rpb, w_router, b_router, w1, b1, w2, b2):
    p = dict(norm1=norm1, norm2=norm2, w_in=w_in, w_out=w_out, a_q_gain=a_q_gain,
             a_k_gain=a_k_gain, a_lambda=a_lambda, a_sub_gain=a_sub_gain, b_dw=b_dw,
             b_dw_bias=b_dw_bias, b_ln_g=b_ln_g, b_ln_b=b_ln_b, c_conv=c_conv,
             d_q_gain=d_q_gain, d_k_gain=d_k_gain, d_rpb=d_rpb, w_router=w_router,
             b_router=b_router, w1=w1, b1=b1, w2=w2, b2=b2)
    bp = c_prompt.shape[0]
    bs = c_sample.shape[0]
    rows = -(-(bp + bs) // 8) * 8
    c_all = jnp.concatenate([c_prompt, c_sample,
                             jnp.zeros((rows - bp - bs, c_prompt.shape[1]), F32)], axis=0)
    mods = _ada(c_all, w_ada, b_ada)
    y_prompt = _trunk(x_prompt, mods[:, :bp], p)
    y_sample = _trunk(x_sample, mods[:, bp:bp + bs], p)
    return (y_prompt, y_sample)
```

```python
import functools
import math

import numpy as np
import jax
import jax.numpy as jnp
from jax import lax
from jax.experimental import pallas as pl
from jax.experimental.pallas import tpu as pltpu

F32 = jnp.float32
BF16 = jnp.bfloat16

D_MODEL = 1024
GROUP_W = 256
A_HEADS = 4
A_VDIM = 64
A_QK = 32
B_KERNEL = 31
C_KERNEL = 3
D_HEADS = 4
D_HDIM = 64
GRID_W = 64
WIN_ROWS = 8
WIN_COLS = 16
IN_COLS = 11 * GROUP_W
N_EXPERTS = 32
TOP_K = 4
D_FF = 1024
SWIGLU_ALPHA = 1.702
SWIGLU_LIMIT = 7.0
N_MOD = 6
NORM_EPS = 1e-6
LN_EPS = 1e-5

LOG2E = 1.4426950408889634
NEG_BIG = -1e30
LANES = 128
A_AUG = 16
A_KDIM = A_QK + A_AUG
A_POS_SPLIT = 256
HALO = 16
D_QROWS = 4
D_KROWS = 12
D_TQ = D_QROWS * GRID_W
D_TK = D_KROWS * GRID_W
MOE_TM = 512
RANK_CHUNK = 128
VMEM_LIMIT = 56 * 1024 * 1024


def _cparams(sems, vmem=VMEM_LIMIT):
    return pltpu.CompilerParams(dimension_semantics=sems, vmem_limit_bytes=vmem)


def _sigmoid(x):
    return 1.0 / (1.0 + jnp.exp(-x))


def _ada_kernel(c_ref, w_ref, b_ref, o_ref):
    c = c_ref[...]
    s = c * _sigmoid(c)
    o_ref[0] = jnp.dot(s.astype(BF16), w_ref[0].astype(BF16),
                       preferred_element_type=F32) + b_ref[0]


def _ada(c_all, w_ada, b_ada):
    depth, d, n6 = w_ada.shape
    rows = c_all.shape[0]
    tn = 1536
    return pl.pallas_call(
        _ada_kernel,
        grid=(depth, n6 // tn),
        in_specs=[pl.BlockSpec((rows, d), lambda l, j: (0, 0)),
                  pl.BlockSpec((1, d, tn), lambda l, j: (l, 0, j)),
                  pl.BlockSpec((1, 1, tn), lambda l, j: (l, 0, j))],
        out_specs=pl.BlockSpec((1, rows, tn), lambda l, j: (l, 0, j)),
        out_shape=jax.ShapeDtypeStruct((depth, rows, n6), F32),
        compiler_params=_cparams(("parallel", "parallel")),
        name="ada_mod",
    )(c_all, w_ada, b_ada.reshape(depth, 1, n6))


def _inproj_kernel(x_ref, g_ref, sh_ref, sc_ref, w_ref, g32_ref, g64_ref,
                   aq_ref, ak_ref, dq_ref, dk_ref, selq_ref, selv_ref, vone_ref,
                   qa_ref, kt_ref, va_ref, ub_ref, uc_ref, qd_ref, kd_ref, vd_ref):
    x = x_ref[...]
    ms = jnp.mean(x * x, axis=-1, keepdims=True)
    h = x * lax.rsqrt(ms + NORM_EPS) * g_ref[...]
    h = h * (1.0 + sc_ref[...]) + sh_ref[...]
    hb = h.astype(BF16)

    def proj(lo, hi):
        return jnp.dot(hb, w_ref[:, lo:hi], preferred_element_type=F32)

    def group_norm(u, gmat_ref, gain_ref, post):
        u2 = u * u
        u2_hi = u2.astype(BF16)
        u2_lo = (u2 - u2_hi.astype(F32)).astype(BF16)
        gmat = gmat_ref[...]
        gms = (jnp.dot(u2_hi, gmat, preferred_element_type=F32)
               + jnp.dot(u2_lo, gmat, preferred_element_type=F32))
        return u * lax.rsqrt(gms + NORM_EPS) * (gain_ref[...] * post)

    w = GROUP_W
    a_scale = (A_QK ** -0.5) * LOG2E
    d_scale = (D_HDIM ** -0.5) * LOG2E
    qa = group_norm(proj(0, w), g32_ref, aq_ref, a_scale).astype(BF16)
    qa_ref[...] = jnp.dot(qa, selq_ref[...], preferred_element_type=F32).astype(BF16)
    kt_ref[...] = group_norm(proj(w, 2 * w), g32_ref, ak_ref, 1.0).T.astype(BF16)
    va = proj(2 * w, 3 * w).astype(BF16)
    va_ref[...] = (jnp.dot(va, selv_ref[...], preferred_element_type=F32)
                   + vone_ref[...]).astype(BF16)
    ub_ref[...] = proj(3 * w, 5 * w)
    uc_ref[...] = proj(5 * w, 8 * w)
    qd_ref[...] = group_norm(proj(8 * w, 9 * w), g64_ref, dq_ref, d_scale).astype(BF16)
    kd_ref[...] = group_norm(proj(9 * w, 10 * w), g64_ref, dk_ref, 1.0).astype(BF16)
    vd_ref[...] = proj(10 * w, 11 * w).astype(BF16)


def _group_mean_matrix(group):
    idx = np.arange(GROUP_W) // group
    return jnp.asarray((idx[:, None] == idx[None, :]).astype(np.float32) / group, dtype=BF16)


def _inproj(x2, t, norm_g, mod4, w_in, aq, ak, dq, dk, tm=512):
    n, d = x2.shape
    tm = min(tm, t)
    tpb = t // tm
    row = lambda i: (i, 0)
    const = lambda i: (0, 0)
    w = GROUP_W
    mod_spec = lambda k: pl.BlockSpec((None, None, 1, d), lambda i: (i // tpb, k, 0, 0))
    n_maps = 2 * A_HEADS
    chan = np.arange(w)
    selq = np.zeros((w, n_maps * LANES), np.float32)
    selq[chan, LANES * (chan // A_QK) + chan % A_QK] = 1.0
    selv = np.zeros((w, A_HEADS * LANES), np.float32)
    selv[chan, LANES * (chan // A_VDIM) + chan % A_VDIM] = 1.0
    vone = np.zeros((1, A_HEADS * LANES), np.float32)
    vone[0, LANES * np.arange(A_HEADS) + A_VDIM] = 1.0
    out_cols = [(n_maps * LANES, BF16), None, (A_HEADS * LANES, BF16), (2 * w, F32),
                (3 * w, F32), (w, BF16), (w, BF16), (w, BF16)]
    out_specs = [pl.BlockSpec((w, tm), lambda i: (0, i)) if c is None
                 else pl.BlockSpec((tm, c[0]), row) for c in out_cols]
    out_shape = [jax.ShapeDtypeStruct((w, n), BF16) if c is None
                 else jax.ShapeDtypeStruct((n, c[0]), c[1]) for c in out_cols]
    return pl.pallas_call(
        _inproj_kernel,
        grid=(n // tm,),
        in_specs=[pl.BlockSpec((tm, d), row),
                  pl.BlockSpec((1, d), const),
                  mod_spec(0), mod_spec(1),
                  pl.BlockSpec((d, IN_COLS), const),
                  pl.BlockSpec((w, w), const), pl.BlockSpec((w, w), const),
                  pl.BlockSpec((1, w), const), pl.BlockSpec((1, w), const),
                  pl.BlockSpec((1, w), const), pl.BlockSpec((1, w), const),
                  pl.BlockSpec((w, n_maps * LANES), const),
                  pl.BlockSpec((w, A_HEADS * LANES), const),
                  pl.BlockSpec((1, A_HEADS * LANES), const)],
        out_specs=out_specs,
        out_shape=out_shape,
        compiler_params=_cparams(("parallel",)),
        name="inproj",
    )(x2, norm_g, mod4, mod4, w_in, _group_mean_matrix(A_QK), _group_mean_matrix(D_HDIM),
      aq, ak, dq, dk, jnp.asarray(selq, BF16), jnp.asarray(selv, BF16), jnp.asarray(vone, F32))


def _bf16_pieces(x, n=3):
    out = []
    for _ in range(n):
        piece = float(np.asarray(x, dtype=BF16).astype(np.float32))
        out.append(piece)
        x = x - piece
    return out


def _alibi_slope(h):
    slope = 2.0 ** (-8.0 * (h + 1) / A_HEADS)
    assert math.log2(slope) == round(math.log2(slope)), "slopes must be powers of two"
    return slope


def _a_constants(tk):
    pos = np.arange(tk)
    lo = (pos % A_POS_SPLIT).astype(np.float32)
    hi = (pos - pos % A_POS_SPLIT).astype(np.float32)
    assert hi.max() / A_POS_SPLIT <= A_POS_SPLIT
    pieces = _bf16_pieces(LOG2E)
    k_rows = np.zeros((A_AUG, tk), np.float32)
    k_rows[0:3] = lo
    k_rows[3:6] = hi
    for r, piece in enumerate(pieces):
        k_rows[6 + r] = piece
        k_rows[9 + r] = piece
    q_cols = np.zeros((A_HEADS, 2, tk, A_KDIM), np.float32)
    for h in range(A_HEADS):
        slope = _alibi_slope(h)
        for var, sign in enumerate((1.0, -1.0)):
            for r, piece in enumerate(pieces):
                q_cols[h, var, :, A_QK + r] = -sign * slope * piece
                q_cols[h, var, :, A_QK + 3 + r] = -sign * slope * piece
                q_cols[h, var, :, A_QK + 6 + r] = sign * slope * lo
                q_cols[h, var, :, A_QK + 9 + r] = sign * slope * hi
    return jnp.asarray(q_cols, BF16), jnp.asarray(k_rows, BF16)


def _attn_a_kernel(q_ref, kt_ref, v_ref, qc_ref, kr_ref, lam_ref, sg_ref, o_ref,
                   qa_sc, kt_sc, m_sc, acc_sc, *, tq, tk):
    i = pl.program_id(1)
    j = pl.program_id(2)
    nj = pl.num_programs(2)
    n_maps = 2 * A_HEADS
    jd = (i * tq) // tk

    @pl.when(j == 0)
    def _init():
        for hc in range(n_maps):
            base = q_ref[0, :, LANES * hc:LANES * hc + A_KDIM].astype(F32)
            for var in range(2):
                qa_sc[hc, var] = (base + qc_ref[hc // 2, var].astype(F32)).astype(BF16)
            kt_sc[hc, A_QK:A_KDIM, :] = kr_ref[...]
        m_sc[...] = jnp.full(m_sc.shape, NEG_BIG, F32)
        acc_sc[...] = jnp.zeros(acc_sc.shape, F32)

    for hc in range(n_maps):
        kt_sc[hc, 0:A_QK, :] = kt_ref[A_QK * hc:A_QK * (hc + 1), :]

    def softmax_step(hc, s, rho):
        m_prev = m_sc[hc]
        m_new = jnp.maximum(m_prev, jnp.max(s, axis=-1, keepdims=True) + rho)
        alpha = jnp.exp2(m_prev - m_new)
        mu = m_new - rho
        p = jnp.exp2(s - jnp.concatenate([mu] * (tk // LANES), axis=1))
        h = hc // 2
        acc_sc[hc] = alpha * acc_sc[hc] + jnp.dot(p.astype(BF16),
                                                  v_ref[0, :, LANES * h:LANES * (h + 1)],
                                                  preferred_element_type=F32)
        m_sc[hc] = m_new

    @pl.when(j != jd)
    def _off_diagonal():
        var = jnp.where(j > jd, 0, 1)
        tile_dist = (jnp.abs(j - jd) * tk).astype(F32)
        for hc in range(n_maps):
            s = jnp.dot(qa_sc[hc, var], kt_sc[hc], preferred_element_type=F32)
            softmax_step(hc, s, tile_dist * (-_alibi_slope(hc // 2) * LOG2E))

    @pl.when(j == jd)
    def _diagonal():
        for hc in range(n_maps):
            kt = kt_sc[hc]
            s = jnp.minimum(jnp.dot(qa_sc[hc, 0], kt, preferred_element_type=F32),
                            jnp.dot(qa_sc[hc, 1], kt, preferred_element_type=F32))
            softmax_step(hc, s, 0.0)

    @pl.when(j == nj - 1)
    def _finish():
        lv = lam_ref[...]
        lambda_init = lv[4:5, 0:1]
        lam = (jnp.exp(jnp.sum(lv[0:1] * lv[1:2], axis=-1, keepdims=True))
               - jnp.exp(jnp.sum(lv[2:3] * lv[3:4], axis=-1, keepdims=True)) + lambda_init)
        lane = lax.broadcasted_iota(jnp.int32, (tq, 2 * A_VDIM), 1)
        heads = []
        for h in range(A_HEADS):
            a1 = acc_sc[2 * h]
            a2 = acc_sc[2 * h + 1]
            l1 = jnp.sum(jnp.where(lane == A_VDIM, a1, 0.0), axis=-1, keepdims=True)
            l2 = jnp.sum(jnp.where(lane == A_VDIM, a2, 0.0), axis=-1, keepdims=True)
            o = jnp.where(lane < A_VDIM, a1 / l1 - lam * (a2 / l2), 0.0)
            ms = jnp.sum(o * o, axis=-1, keepdims=True) * (1.0 / A_VDIM)
            heads.append(o * lax.rsqrt(ms + NORM_EPS) * (sg_ref[...] * (1.0 - lambda_init)))
        o_ref[0, :, 0:128] = heads[0] + pltpu.roll(heads[1], A_VDIM, 1)
        o_ref[0, :, 128:256] = heads[2] + pltpu.roll(heads[3], A_VDIM, 1)


def _attn_a(q_sel, kt, v_aug, lam, sub_gain, lambda_init, tq=1024, tk=1024):
    b, t, _ = q_sel.shape
    lam = jnp.concatenate([lam.astype(F32), jnp.full((1, A_QK), lambda_init, F32)], axis=0)
    w = GROUP_W
    tq = min(tq, t)
    tk = min(tk, t)
    assert tk % tq == 0 and tk % A_POS_SPLIT == 0 and t % tk == 0
    n_maps = 2 * A_HEADS
    nkt = t // tk
    qpt = tk // tq
    q_cols, k_rows = _a_constants(tk)
    return pl.pallas_call(
        functools.partial(_attn_a_kernel, tq=tq, tk=tk),
        grid=(b, t // tq, nkt),
        in_specs=[pl.BlockSpec((1, tq, n_maps * LANES), lambda bi, i, j: (bi, i, 0)),
                  pl.BlockSpec((w, tk), lambda bi, i, j: (0, bi * nkt + j)),
                  pl.BlockSpec((1, tk, A_HEADS * LANES), lambda bi, i, j: (bi, j, 0)),
                  pl.BlockSpec((A_HEADS, 2, tq, A_KDIM), lambda bi, i, j: (0, 0, i % qpt, 0)),
                  pl.BlockSpec((A_AUG, tk), lambda bi, i, j: (0, 0)),
                  pl.BlockSpec((5, A_QK), lambda bi, i, j: (0, 0)),
                  pl.BlockSpec((1, 2 * A_VDIM), lambda bi, i, j: (0, 0))],
        out_specs=pl.BlockSpec((1, tq, w), lambda bi, i, j: (bi, i, 0)),
        out_shape=jax.ShapeDtypeStruct((b, t, w), F32),
        scratch_shapes=[pltpu.VMEM((n_maps, 2, tq, A_KDIM), BF16),
                        pltpu.VMEM((n_maps, A_KDIM, tk), BF16),
                        pltpu.VMEM((n_maps, tq, LANES), F32),
                        pltpu.VMEM((n_maps, tq, 2 * A_VDIM), F32)],
        compiler_params=_cparams(("parallel", "parallel", "arbitrary")),
        name="attn_a",
    )(q_sel, kt, v_aug, q_cols, k_rows, lam, sub_gain)


def _conv_kernel(ub_ref, ubp_ref, ubn_ref, uc_ref, ucp_ref, ucn_ref,
                 wdw_ref, bdw_ref, lng_ref, lnb_ref, wc_ref,
                 ob_ref, oc_ref, gext, pext, *, tt, rc):
    i = pl.program_id(1)
    n = pl.num_programs(1)
    has_prev = jnp.where(i > 0, 1.0, 0.0)
    has_next = jnp.where(i < n - 1, 1.0, 0.0)
    w = GROUP_W

    def glu(u):
        return u[:, :w] * _sigmoid(u[:, w:])

    def gated(u):
        return u[:, w:2 * w] * u[:, 2 * w:]

    gext[0:HALO, :] = glu(ubp_ref[0]) * has_prev
    gext[HALO:HALO + tt, :] = glu(ub_ref[0])
    gext[HALO + tt:2 * HALO + tt, :] = glu(ubn_ref[0]) * has_next
    pext[0:HALO, :] = gated(ucp_ref[0]) * has_prev
    pext[HALO:HALO + tt, :] = gated(uc_ref[0])
    pext[HALO + tt:2 * HALO + tt, :] = gated(ucn_ref[0]) * has_next

    pad_b = (B_KERNEL - 1) // 2
    pad_c = (C_KERNEL - 1) // 2
    for r0 in range(0, tt, rc):
        acc = jnp.zeros((rc, w), F32) + bdw_ref[...]
        for k in range(B_KERNEL):
            off = HALO + r0 + k - pad_b
            acc = acc + wdw_ref[k:k + 1, :] * gext[off:off + rc, :]
        mu = jnp.mean(acc, axis=-1, keepdims=True)
        cen = acc - mu
        var = jnp.mean(cen * cen, axis=-1, keepdims=True)
        y = cen * lax.rsqrt(var + LN_EPS) * lng_ref[...] + lnb_ref[...]
        ob_ref[0, r0:r0 + rc, :] = y * _sigmoid(y)

        accc = jnp.zeros((rc, w), F32)
        for k in range(C_KERNEL):
            off = HALO + r0 + k - pad_c
            accc = accc + wc_ref[k:k + 1, :] * pext[off:off + rc, :]
        oc_ref[0, r0:r0 + rc, :] = uc_ref[0, r0:r0 + rc, 0:w] * accc


def _conv(ub, uc, w_dw, b_dw, ln_g, ln_b, w_c, tt=256, rc=64):
    b, t, _ = ub.shape
    tt = min(tt, t)
    w = GROUP_W
    hb = tt // HALO
    last = t // HALO - 1
    cur = lambda bi, i: (bi, i, 0)
    prev = lambda bi, i: (bi, jnp.maximum(i * hb - 1, 0), 0)
    nxt = lambda bi, i: (bi, jnp.minimum((i + 1) * hb, last), 0)
    const = lambda bi, i: (0, 0)
    return pl.pallas_call(
        functools.partial(_conv_kernel, tt=tt, rc=rc),
        grid=(b, t // tt),
        in_specs=[pl.BlockSpec((1, tt, 2 * w), cur),
                  pl.BlockSpec((1, HALO, 2 * w), prev),
                  pl.BlockSpec((1, HALO, 2 * w), nxt),
                  pl.BlockSpec((1, tt, 3 * w), cur),
                  pl.BlockSpec((1, HALO, 3 * w), prev),
                  pl.BlockSpec((1, HALO, 3 * w), nxt),
                  pl.BlockSpec((B_KERNEL, w), const),
                  pl.BlockSpec((1, w), const), pl.BlockSpec((1, w), const),
                  pl.BlockSpec((1, w), const),
                  pl.BlockSpec((C_KERNEL, w), const)],
        out_specs=[pl.BlockSpec((1, tt, w), cur), pl.BlockSpec((1, tt, w), cur)],
        out_shape=[jax.ShapeDtypeStruct((b, t, w), F32), jax.ShapeDtypeStruct((b, t, w), F32)],
        scratch_shapes=[pltpu.VMEM((tt + 2 * HALO, w), F32), pltpu.VMEM((tt + 2 * HALO, w), F32)],
        compiler_params=_cparams(("parallel", "parallel")),
        name="conv_bc",
    )(ub, ub, ub, uc, uc, uc, w_dw, b_dw, ln_g, ln_b, w_c)


def _attn_d_kernel(q_ref, k_ref, v_ref, b_ref, o_ref, *, t):
    g = pl.program_id(1)
    start = pl.multiple_of(jnp.clip((g - 1) * D_TQ, 0, t - D_TK), D_TQ)
    q = q_ref[0].astype(F32)
    k = k_ref[0, pl.ds(start, D_TK), :]
    v = v_ref[0, pl.ds(start, D_TK), :]
    lane_h = lax.broadcasted_iota(jnp.int32, (D_TQ, GROUP_W), 1) // D_HDIM
    out = jnp.zeros((D_TQ, GROUP_W), F32)
    for h in range(D_HEADS):
        qm = jnp.where(lane_h == h, q, 0.0).astype(BF16)
        s = lax.dot_general(qm, k, (((1,), (1,)), ((), ())),
                            preferred_element_type=F32) + b_ref[0, h]
        m = jnp.max(s, axis=-1, keepdims=True)
        p = jnp.exp2(s - m)
        l = jnp.sum(p, axis=-1, keepdims=True)
        o = jnp.dot(p.astype(BF16), v, preferred_element_type=F32)
        out = jnp.where(lane_h == h, o / l, out)
    o_ref[0] = out


def _d_tables():
    qc = np.arange(GRID_W)
    kc = np.arange(GRID_W)
    cs = np.clip(qc - WIN_COLS // 2, 0, GRID_W - WIN_COLS)
    ok_c = (kc[None, :] >= cs[:, None]) & (kc[None, :] < cs[:, None] + WIN_COLS)
    crel = kc[None, :] - qc[:, None] + WIN_COLS - 1
    n_c = 2 * WIN_COLS - 1
    col_sel = (crel[None] == np.arange(n_c)[:, None, None]).astype(np.float32)
    rows = 3 * D_KROWS
    groups = rows // D_QROWS
    n_r = 2 * WIN_ROWS - 1
    row_sel = np.zeros((3, D_QROWS, D_KROWS, n_r), np.float32)
    ok_r = np.zeros((3, D_QROWS, D_KROWS), bool)
    for var, g in enumerate((0, groups // 2, groups - 1)):
        start_row = int(np.clip(D_QROWS * (g - 1), 0, rows - D_KROWS))
        for a in range(D_QROWS):
            qr = D_QROWS * g + a
            rs = int(np.clip(qr - WIN_ROWS // 2, 0, rows - WIN_ROWS))
            for kb in range(D_KROWS):
                kr = start_row + kb
                if rs <= kr < rs + WIN_ROWS:
                    ok_r[var, a, kb] = True
                    row_sel[var, a, kb, kr - qr + WIN_ROWS - 1] = 1.0
    valid = ok_r[:, :, None, :, None] & ok_c[None, None, :, None, :]
    return col_sel, row_sel, valid.reshape(3, D_TQ, D_TK)


def _d_bias(rpb):
    col_sel, row_sel, valid = _d_tables()
    hp = lax.Precision.HIGHEST
    tab = jnp.einsum('hrc,cqk->hrqk', rpb.astype(F32), col_sel, precision=hp)
    bias = jnp.einsum('vabr,hrqk->vhaqbk', row_sel, tab, precision=hp)
    bias = bias.reshape(3, D_HEADS, D_TQ, D_TK) * LOG2E
    return jnp.where(valid[:, None], bias, NEG_BIG)


def _attn_d(qd, kd, vd, bias):
    b, t, w = qd.shape
    assert t % D_TQ == 0 and t >= D_TK
    groups = t // D_TQ
    variant = lambda bi, g: (jnp.where(g == 0, 0, jnp.where(g == groups - 1, 2, 1)), 0, 0, 0)
    return pl.pallas_call(
        functools.partial(_attn_d_kernel, t=t),
        grid=(b, groups),
        in_specs=[pl.BlockSpec((1, D_TQ, w), lambda bi, g: (bi, g, 0)),
                  pl.BlockSpec((1, t, w), lambda bi, g: (bi, 0, 0)),
                  pl.BlockSpec((1, t, w), lambda bi, g: (bi, 0, 0)),
                  pl.BlockSpec((1, D_HEADS, D_TQ, D_TK), variant)],
        out_specs=pl.BlockSpec((1, D_TQ, w), lambda bi, g: (bi, g, 0)),
        out_shape=jax.ShapeDtypeStruct((b, t, w), F32),
        compiler_params=_cparams(("parallel", "arbitrary")),
        name="attn_d",
    )(qd, kd, vd, bias)


def _outproj_kernel(x_ref, oa_ref, ob_ref, oc_ref, od_ref, w_ref, g1_ref, sh_ref, sc_ref,
                    n2_ref, wr_ref, br_ref, xo_ref, h2_ref, lg_ref):
    w = GROUP_W
    mix = jnp.dot(oa_ref[...].astype(BF16), w_ref[0:w, :], preferred_element_type=F32)
    mix += jnp.dot(ob_ref[...].astype(BF16), w_ref[w:2 * w, :], preferred_element_type=F32)
    mix += jnp.dot(oc_ref[...].astype(BF16), w_ref[2 * w:3 * w, :], preferred_element_type=F32)
    mix += jnp.dot(od_ref[...].astype(BF16), w_ref[3 * w:4 * w, :], preferred_element_type=F32)
    xn = x_ref[...] + g1_ref[...] * mix
    xo_ref[...] = xn
    ms = jnp.mean(xn * xn, axis=-1, keepdims=True)
    h2 = xn * lax.rsqrt(ms + NORM_EPS) * n2_ref[...]
    h2 = h2 * (1.0 + sc_ref[...]) + sh_ref[...]
    h_hi = h2.astype(BF16)
    h_lo = (h2 - h_hi.astype(F32)).astype(BF16)
    wr = wr_ref[...]
    w_hi = wr.astype(BF16)
    w_lo = (wr - w_hi.astype(F32)).astype(BF16)
    lg = (jnp.dot(h_hi, w_hi, preferred_element_type=F32)
          + jnp.dot(h_hi, w_lo, preferred_element_type=F32)
          + jnp.dot(h_lo, w_hi, preferred_element_type=F32))
    lg_ref[...] = lg + br_ref[...]
    h2_ref[...] = h_hi


def _outproj(x2, t, oa, ob, oc, od, w_out, mod4, norm_g, w_router, b_router, tm=512):
    n, d = x2.shape
    tm = min(tm, t)
    tpb = t // tm
    w = GROUP_W
    row = lambda i: (i, 0)
    const = lambda i: (0, 0)
    mod_spec = lambda k: pl.BlockSpec((None, None, 1, d), lambda i: (i // tpb, k, 0, 0))
    return pl.pallas_call(
        _outproj_kernel,
        grid=(n // tm,),
        in_specs=[pl.BlockSpec((tm, d), row),
                  pl.BlockSpec((tm, w), row), pl.BlockSpec((tm, w), row),
                  pl.BlockSpec((tm, w), row), pl.BlockSpec((tm, w), row),
                  pl.BlockSpec((d, d), const),
                  mod_spec(2), mod_spec(3), mod_spec(4),
                  pl.BlockSpec((1, d), const),
                  pl.BlockSpec((d, N_EXPERTS), const),
                  pl.BlockSpec((1, N_EXPERTS), const)],
        out_specs=[pl.BlockSpec((tm, d), row), pl.BlockSpec((tm, d), row),
                   pl.BlockSpec((tm, N_EXPERTS), row)],
        out_shape=[jax.ShapeDtypeStruct((n, d), F32), jax.ShapeDtypeStruct((n, d), BF16),
                   jax.ShapeDtypeStruct((n, N_EXPERTS), F32)],
        compiler_params=_cparams(("parallel",)),
        name="outproj",
    )(x2, oa, ob, oc, od, w_out, mod4, mod4, mod4, norm_g, w_router, b_router)


def _moe_kernel(be_ref, nu_ref, x_ref, w1_ref, b1_ref, w2_ref, b2_ref, y_ref):
    i = pl.program_id(0)

    @pl.when(i < nu_ref[0])
    def _compute():
        h = jnp.dot(x_ref[...], w1_ref[0].astype(BF16), preferred_element_type=F32) + b1_ref[0]
        glu = jnp.minimum(h[:, :D_FF], SWIGLU_LIMIT)
        lin = jnp.clip(h[:, D_FF:], -SWIGLU_LIMIT, SWIGLU_LIMIT)
        act = glu * _sigmoid(SWIGLU_ALPHA * glu) * (lin + 1.0)
        y = jnp.dot(act.astype(BF16), w2_ref[0].astype(BF16),
                    preferred_element_type=F32) + b2_ref[0]
        y_ref[...] = y.astype(y_ref.dtype)

    @pl.when(i >= nu_ref[0])
    def _unused():
        y_ref[...] = jnp.zeros(y_ref.shape, y_ref.dtype)


def _moe(xs, block_expert, n_used, w1, b1, w2, b2, layer, tm):
    n_rows, d = xs.shape
    depth, e, _, f2 = w1.shape
    n_blocks = n_rows // tm
    grid_spec = pltpu.PrefetchScalarGridSpec(
        num_scalar_prefetch=2,
        grid=(n_blocks,),
        in_specs=[pl.BlockSpec((tm, d), lambda i, be, nu: (i, 0)),
                  pl.BlockSpec((None, 1, d, f2), lambda i, be, nu: (layer, be[i], 0, 0)),
                  pl.BlockSpec((None, 1, 1, f2), lambda i, be, nu: (layer, be[i], 0, 0)),
                  pl.BlockSpec((None, 1, f2 // 2, d), lambda i, be, nu: (layer, be[i], 0, 0)),
                  pl.BlockSpec((None, 1, 1, d), lambda i, be, nu: (layer, be[i], 0, 0))],
        out_specs=pl.BlockSpec((tm, d), lambda i, be, nu: (i, 0)),
    )
    return pl.pallas_call(
        _moe_kernel,
        grid_spec=grid_spec,
        out_shape=jax.ShapeDtypeStruct((n_rows, d), BF16),
        compiler_params=_cparams(("arbitrary",)),
        name="moe_experts",
    )(block_expert, n_used, xs, w1, b1.reshape(depth, e, 1, f2), w2, b2.reshape(depth, e, 1, d))


def _route(logits, tm):
    n = logits.shape[0]
    top_val, top_idx = lax.top_k(logits, TOP_K)
    gates = jax.nn.softmax(top_val, axis=-1)
    nk = n * TOP_K
    sel = top_idx[:, :, None] == jnp.arange(N_EXPERTS, dtype=top_idx.dtype)[None, None, :]
    cnt = jnp.any(sel, axis=1).astype(BF16).reshape(n // RANK_CHUNK, RANK_CHUNK, N_EXPERTS)
    tri = jnp.asarray(np.tril(np.ones((RANK_CHUNK, RANK_CHUNK), np.float32), -1), BF16)
    within = jnp.einsum('ij,cjk->cik', tri, cnt, preferred_element_type=F32)
    tot = jnp.sum(cnt.astype(F32), axis=1)
    offs = jnp.cumsum(tot, axis=0) - tot
    ahead = (within + offs[:, None, :]).reshape(n, 1, N_EXPERTS)
    counts = (offs[-1] + tot[-1]).astype(jnp.int32)
    padded = (counts + tm - 1) // tm * tm
    pend = jnp.cumsum(padded)
    pstart = (pend - padded).astype(F32)
    dest = jnp.sum(jnp.where(sel, ahead + pstart[None, None, :], 0.0), axis=2).astype(jnp.int32)
    n_blocks = nk // tm + N_EXPERTS
    n_rows = n_blocks * tm
    row_tok = jnp.zeros((n_rows,), jnp.int32).at[dest.reshape(nk)].set(
        jnp.arange(nk, dtype=jnp.int32) // TOP_K)
    block_start = jnp.arange(n_blocks, dtype=jnp.int32) * tm
    block_expert = jnp.minimum(
        jnp.sum((pend[None, :] <= block_start[:, None]).astype(jnp.int32), axis=1),
        N_EXPERTS - 1).astype(jnp.int32)
    n_used = (pend[-1] // tm).astype(jnp.int32).reshape(1)
    return dest, row_tok, gates, block_expert, n_used


def _combine_kernel(x_ref, y_ref, gt_ref, g2_ref, o_ref):
    reps = x_ref.shape[1] // LANES
    acc = None
    for k in range(TOP_K):
        gate = jnp.concatenate([gt_ref[k]] * reps, axis=1)
        term = gate * y_ref[k].astype(F32)
        acc = term if acc is None else acc + term
    o_ref[...] = x_ref[...] + g2_ref[...] * acc


def _combine(x2, t, yk, gates_b, mod4, tm=512):
    n, d = x2.shape
    tm = min(tm, t)
    tpb = t // tm
    return pl.pallas_call(
        _combine_kernel,
        grid=(n // tm,),
        in_specs=[pl.BlockSpec((tm, d), lambda i: (i, 0)),
                  pl.BlockSpec((TOP_K, tm, d), lambda i: (0, i, 0)),
                  pl.BlockSpec((TOP_K, tm, LANES), lambda i: (0, i, 0)),
                  pl.BlockSpec((None, None, 1, d), lambda i: (i // tpb, 5, 0, 0))],
        out_specs=pl.BlockSpec((tm, d), lambda i: (i, 0)),
        out_shape=jax.ShapeDtypeStruct((n, d), F32),
        compiler_params=_cparams(("parallel",)),
        name="combine",
    )(x2, yk, gates_b, mod4)


def _tile_gain(g, reps):
    return jnp.tile(g.astype(F32), reps).reshape(1, -1)


def _trunk(x, mods, p, moe_tm=MOE_TM):
    b, t, d = x.shape
    n = b * t
    depth = p["norm1"].shape[0]
    x2 = x.reshape(n, d)
    w = GROUP_W
    for l in range(depth):
        lambda_init = 0.8 - 0.6 * math.exp(-0.3 * l)
        mod4 = mods[l].reshape(b, N_MOD, 1, d)
        q_sel, kt, v_aug, ub, uc, qd, kd, vd = _inproj(
            x2, t, p["norm1"][l].reshape(1, d), mod4, p["w_in"][l].astype(BF16),
            _tile_gain(p["a_q_gain"][l], w // A_QK), _tile_gain(p["a_k_gain"][l], w // A_QK),
            _tile_gain(p["d_q_gain"][l], w // D_HDIM), _tile_gain(p["d_k_gain"][l], w // D_HDIM))

        sub_gain = jnp.concatenate([p["a_sub_gain"][l].astype(F32),
                                    jnp.zeros((A_VDIM,), F32)]).reshape(1, 2 * A_VDIM)
        oa = _attn_a(q_sel.reshape(b, t, -1), kt, v_aug.reshape(b, t, -1),
                     p["a_lambda"][l], sub_gain, lambda_init)

        ob, oc = _conv(ub.reshape(b, t, 2 * w), uc.reshape(b, t, 3 * w), p["b_dw"][l],
                       p["b_dw_bias"][l].reshape(1, w), p["b_ln_g"][l].reshape(1, w),
                       p["b_ln_b"][l].reshape(1, w), p["c_conv"][l])

        od = _attn_d(qd.reshape(b, t, w), kd.reshape(b, t, w), vd.reshape(b, t, w),
                     _d_bias(p["d_rpb"][l]))

        x2, h2, logits = _outproj(
            x2, t, oa.reshape(n, w), ob.reshape(n, w), oc.reshape(n, w), od.reshape(n, w),
            p["w_out"][l].astype(BF16), mod4, p["norm2"][l].reshape(1, d),
            p["w_router"][l], p["b_router"][l].reshape(1, N_EXPERTS))

        dest, row_tok, gates, block_expert, n_used = _route(logits, moe_tm)
        xs = h2[row_tok]
        y = _moe(xs, block_expert, n_used, p["w1"], p["b1"], p["w2"], p["b2"], l, moe_tm)
        yk = y[dest.T]
        gates_b = jnp.broadcast_to(gates.T[:, :, None], (TOP_K, n, LANES))
        x2 = _combine(x2, t, yk, gates_b, mod4)
    return x2.reshape(b, t, d)


def kernel(x_prompt, x_sample, c_prompt, c_sample, norm1, norm2, w_ada, b_ada, w_in, w_out, a_q_gain, a_k_gain, a_lambda, a_sub_gain, b_dw, b_dw_bias, b_ln_g, b_ln_b, c_conv, d_q_gain, d_k_gain, d_rpb, w_router, b_router, w1, b1, w2, b2):
    p = dict(norm1=norm1, norm2=norm2, w_in=w_in, w_out=w_out, a_q_gain=a_q_gain,
             a_k_gain=a_k_gain, a_lambda=a_lambda, a_sub_gain=a_sub_gain, b_dw=b_dw,
             b_dw_bias=b_dw_bias, b_ln_g=b_ln_g, b_ln_b=b_ln_b, c_conv=c_conv,
             d_q_gain=d_q_gain, d_k_gain=d_k_gain, d_rpb=d_rpb, w_router=w_router,
             b_router=b_router, w1=w1, b1=b1, w2=w2, b2=b2)
    bp = c_prompt.shape[0]
    bs = c_sample.shape[0]
    rows = -(-(bp + bs) // 8) * 8
    c_all = jnp.concatenate([c_prompt, c_sample,
                             jnp.zeros((rows - bp - bs, c_prompt.shape[1]), F32)], axis=0)
    mods = _ada(c_all, w_ada, b_ada)
    y_prompt = _trunk(x_prompt, mods[:, :bp], p)
    y_sample = _trunk(x_sample, mods[:, bp:bp + bs], p)
    return (y_prompt, y_sample)
```
